```python
import functools
import jax, jax.numpy as jnp
from jax import lax
import numpy as np

D_MODEL = 2048
BATCH = 4
SEQ = 4096
DEPTH = 2
DEC_BATCH = 128
DEC_SEQ = 4
PAST_LEN = 16384
PAGE_SIZE = 128

H_A = 8
DK_A = 128
DV_A = 128
CONV_W = 4
GDN_CHUNK = 64
A_QKV = 2 * H_A * DK_A + H_A * DV_A
H_B = 8
KVH_B = 2
HD_B = 128
HI_B = 16
DI_B = 64
TOPK_MAX = 256
D_C = 1024
NB_C = 8
BW_C = D_C // NB_C
RG_C = 8.0
H_D = 8
DN_D = 128
DR_D = 64
DV_D = 128
R_KV = 256
ROPE_THETA = 10000.0
D_FF = 5632
Q_BLOCK = 128
LN_EPS = 1e-5
RMS_EPS = 1e-6
DN_ALPHA = (2 * DEPTH) ** 0.25
DN_BETA = (8 * DEPTH) ** -0.25
AB_SPLITS = (A_QKV, H_A * DV_A, H_A, H_A, H_B * HD_B, KVH_B * HD_B, KVH_B * HD_B, HI_B * DI_B, DI_B, HI_B)
CD_SPLITS = (D_C, D_C, H_D * (DN_D + DR_D), R_KV, DR_D)
MIX_AB = H_A * DV_A + H_B * HD_B
MIX_CD = D_C + H_D * DV_D
IDX_SCALE = (HI_B * DI_B) ** -0.5
ATT_SCALE_B = HD_B ** -0.5
ATT_SCALE_D = (DN_D + DR_D) ** -0.5
F32 = jnp.float32

kernel_name = "hybrid_gdn_dsa_rglru_mla_step"


def _split(h, sizes):
    idx = [int(i) for i in np.cumsum(sizes)[:-1]]
    return jnp.split(h, idx, axis=-1)


def layer_norm(x, g, b):
    xf = x.astype(F32)
    mu = xf.mean(-1, keepdims=True)
    var = jnp.square(xf - mu).mean(-1, keepdims=True)
    return ((xf - mu) * lax.rsqrt(var + LN_EPS) * g + b).astype(x.dtype)


def rms_norm(x, g):
    xf = x.astype(F32)
    return xf * lax.rsqrt(jnp.mean(xf * xf, -1, keepdims=True) + RMS_EPS) * g


def l2_norm(x):
    xf = x.astype(F32)
    return xf * lax.rsqrt(jnp.sum(xf * xf, -1, keepdims=True) + RMS_EPS)


def post_norm(x, f, g, b):
    return layer_norm(DN_ALPHA * x + f, g, b)


def swiglu(x, wg, wu, wd):
    return (jax.nn.silu(x @ wg) * (x @ wu)) @ wd


def causal_conv(x, buf, w):
    t = x.shape[1]
    xp = jnp.concatenate([buf.astype(x.dtype), x], axis=1)
    y = sum(xp[:, j:j + t] * w[j] for j in range(CONV_W))
    return y, xp[:, t:]


def rope(x, pos):
    half = DR_D // 2
    inv = ROPE_THETA ** (-jnp.arange(half, dtype=F32) / half)
    ang = pos.astype(F32)[:, None] * inv
    ang = ang.reshape(ang.shape[0], *([1] * (x.ndim - 3)), half)
    cos, sin = jnp.cos(ang), jnp.sin(ang)
    xf = x.astype(F32)
    x1, x2 = xf[..., :half], xf[..., half:]
    return jnp.concatenate([x1 * cos - x2 * sin, x2 * cos + x1 * sin], -1).astype(x.dtype)


def _chunk(a, n, c):
    a = a.astype(F32)
    a = jnp.pad(a, [(0, 0), (0, n * c - a.shape[1])] + [(0, 0)] * (a.ndim - 2))
    a = a.reshape(a.shape[0], n, c, *a.shape[2:])
    return jnp.swapaxes(jnp.swapaxes(a, 0, 1), 2, 3)


def gated_delta(q, k, v, g, beta, s0):
    bsz, t = q.shape[:2]
    c = min(GDN_CHUNK, t)
    n = -(-t // c)
    qc = _chunk(q * DK_A ** -0.5, n, c)
    kc, vc = _chunk(k, n, c), _chunk(v, n, c)
    gc, bc = _chunk(g, n, c), _chunk(beta, n, c)
    gcum = jnp.cumsum(gc, axis=-1)
    incl = jnp.tril(jnp.ones((c, c), bool))
    strict = jnp.tril(jnp.ones((c, c), bool), -1)
    decay = jnp.exp(jnp.where(incl, gcum[..., :, None] - gcum[..., None, :], -jnp.inf))
    kb = kc * bc[..., None]
    lmat = jnp.where(strict, jnp.einsum('nbhid,nbhjd->nbhij', kb, kc) * decay, 0.0)
    eye = jnp.eye(c, dtype=F32)
    tmat = lax.linalg.triangular_solve(eye + lmat, jnp.broadcast_to(eye, lmat.shape),
                                       left_side=True, lower=True, unit_diagonal=True)
    u = tmat @ (vc * bc[..., None])
    w = tmat @ (kb * jnp.exp(gcum)[..., None])
    qk = jnp.einsum('nbhid,nbhjd->nbhij', qc, kc) * decay
    qg = qc * jnp.exp(gcum)[..., None]
    kd = kc * jnp.exp(gcum[..., -1:] - gcum)[..., None]
    glast = jnp.exp(gcum[..., -1])

    def step(s, xs):
        u_i, w_i, qk_i, qg_i, kd_i, gl_i = xs
        v_new = u_i - jnp.einsum('bhcd,bhde->bhce', w_i, s)
        o = jnp.einsum('bhcd,bhde->bhce', qg_i, s) + jnp.einsum('bhij,bhje->bhie', qk_i, v_new)
        s = s * gl_i[..., None, None] + jnp.einsum('bhcd,bhce->bhde', kd_i, v_new)
        return s, o

    s, o = lax.scan(step, s0.astype(F32), (u, w, qk, qg, kd, glast))
    o = jnp.swapaxes(jnp.swapaxes(o, 2, 3), 0, 1).reshape(bsz, n * c, H_A, DV_A)[:, :t]
    return o, s


def index_scores(qi, keys, wi):
    s = jax.nn.relu(jnp.einsum('bthd,bsd->bths', qi, keys, preferred_element_type=F32))
    return jnp.einsum('bth,bths->bts', wi.astype(F32), s)


def gathered_attention(q, kg, vg, valid):
    bsz, t = q.shape[:2]
    qg = q.reshape(bsz, t, KVH_B, H_B // KVH_B, HD_B)
    s = jnp.einsum('btkgd,btnkd->btkgn', qg, kg, preferred_element_type=F32) * ATT_SCALE_B
    s = jnp.where(valid[:, :, None, None, :], s, -jnp.inf)
    p = jax.nn.softmax(s, axis=-1)
    o = jnp.einsum('btkgn,btnkd->btkgd', p.astype(vg.dtype), vg)
    return o.reshape(bsz, t, H_B, HD_B)


def dsa_prompt(q, k, v, qi, ki, wi):
    bsz, s_len = q.shape[:2]
    ksel = min(TOPK_MAX, s_len // 4)
    qb = min(Q_BLOCK, s_len)
    nb = s_len // qb
    key_pos = jnp.arange(s_len)
    take = jax.vmap(lambda a, i: a[i])

    def blocks(a):
        return jnp.moveaxis(a.reshape(bsz, nb, qb, *a.shape[2:]), 1, 0)

    def one_block(args):
        q_blk, qi_blk, wi_blk, t0 = args
        sc = index_scores(qi_blk, ki, wi_blk)
        q_pos = t0 + jnp.arange(qb)
        sc = jnp.where(key_pos[None, None, :] <= q_pos[None, :, None], sc, -jnp.inf)
        val, idx = lax.top_k(sc, ksel)
        return gathered_attention(q_blk, take(k, idx), take(v, idx), jnp.isfinite(val))

    out = lax.map(one_block, (blocks(q), blocks(qi), blocks(wi), jnp.arange(nb) * qb))
    return jnp.moveaxis(out, 0, 1).reshape(bsz, s_len, H_B, HD_B)


def dsa_sample(q, k, v, qi, ki, wi, cache_k, cache_v, cache_kidx, page_table):
    bsz, t = q.shape[:2]
    past = page_table.shape[1] * PAGE_SIZE
    ksel = min(TOPK_MAX, (past + t) // 4)
    sc_past = lax.map(lambda pt: index_scores(qi, cache_kidx[pt], wi), page_table.T)
    sc_past = jnp.moveaxis(sc_past, 0, 2).reshape(bsz, t, past)
    causal = jnp.tril(jnp.ones((t, t), bool))
    sc_new = jnp.where(causal, index_scores(qi, ki, wi), -jnp.inf)
    val, idx = lax.top_k(jnp.concatenate([sc_past, sc_new], -1), ksel)
    bi = jnp.arange(bsz)[:, None, None]
    is_past = (idx < past)[..., None, None]
    pidx = jnp.minimum(idx, past - 1)
    phys = page_table[bi, pidx // PAGE_SIZE]
    off = pidx % PAGE_SIZE
    nidx = jnp.clip(idx - past, 0, t - 1)
    kg = jnp.where(is_past, cache_k[phys, off], k[bi, nidx])
    vg = jnp.where(is_past, cache_v[phys, off], v[bi, nidx])
    return gathered_attention(q, kg, vg, jnp.isfinite(val))


def mixer_ab(x, conv_buf, rec_state, sparse_attn, w_in_ab, conv_w_a, a_log_a, dt_bias_a, norm_g_a, w_out_ab):
    bsz, t, _ = x.shape
    h = x @ w_in_ab
    qkv_pre, z, b_gate, a_gate, q_b, k_b, v_b, qi_b, ki_b, wi_b = _split(h, AB_SPLITS)
    qkv, conv_new = causal_conv(qkv_pre, conv_buf, conv_w_a)
    qa, ka, va = _split(jax.nn.silu(qkv), (H_A * DK_A, H_A * DK_A, H_A * DV_A))
    qa = l2_norm(qa.reshape(bsz, t, H_A, DK_A))
    ka = l2_norm(ka.reshape(bsz, t, H_A, DK_A))
    va = va.reshape(bsz, t, H_A, DV_A)
    beta = jax.nn.sigmoid(b_gate.astype(F32))
    g = -jnp.exp(a_log_a.astype(F32)) * jax.nn.softplus(a_gate.astype(F32) + dt_bias_a)
    o_a, s_new = gated_delta(qa, ka, va, g, beta, rec_state)
    o_a = rms_norm(o_a, norm_g_a) * jax.nn.silu(z.reshape(bsz, t, H_A, DV_A).astype(F32))
    q_b = q_b.reshape(bsz, t, H_B, HD_B)
    k_b = k_b.reshape(bsz, t, KVH_B, HD_B)
    v_b = v_b.reshape(bsz, t, KVH_B, HD_B)
    qi_b = qi_b.reshape(bsz, t, HI_B, DI_B)
    o_b = sparse_attn(q_b, k_b, v_b, qi_b, ki_b, wi_b * IDX_SCALE)
    o = jnp.concatenate([o_a.reshape(bsz, t, -1).astype(x.dtype), o_b.reshape(bsz, t, -1).astype(x.dtype)], -1)
    return o @ w_out_ab, conv_new, s_new, k_b, v_b, ki_b


def rglru_branch(xb, yb, conv_buf, h0, conv_w_c, conv_b_c, w_rg_a, b_rg_a, w_rg_x, b_rg_x, lambda_c):
    bsz, t, _ = xb.shape
    xc, conv_new = causal_conv(xb, conv_buf, conv_w_c)
    xc = (xc + conv_b_c).astype(F32)
    xr = xc.reshape(bsz, t, NB_C, BW_C)
    r = jax.nn.sigmoid(jnp.einsum('btnd,nde->btne', xr, w_rg_a.astype(F32)).reshape(bsz, t, D_C) + b_rg_a)
    i = jax.nn.sigmoid(jnp.einsum('btnd,nde->btne', xr, w_rg_x.astype(F32)).reshape(bsz, t, D_C) + b_rg_x)
    log_a = -RG_C * r * jax.nn.softplus(-lambda_c.astype(F32))
    a = jnp.exp(log_a)
    bterm = jnp.sqrt(-jnp.expm1(2.0 * log_a)) * (i * xc)
    bterm = bterm.at[:, 0].add(a[:, 0] * h0.astype(F32))

    def comb(lhs, rhs):
        return lhs[0] * rhs[0], rhs[0] * lhs[1] + rhs[1]

    _, h = lax.associative_scan(comb, (a, bterm), axis=1)
    return h * jax.nn.gelu(yb.astype(F32)), conv_new, h[:, -1]


def mla_prompt(qn, qr, ckv, kr, w_uk, w_uv):
    bsz, s_len = qn.shape[:2]
    k_nope = jnp.einsum('bsr,rhd->bshd', ckv, w_uk)
    v = jnp.einsum('bsr,rhd->bshd', ckv, w_uv)
    qb = min(Q_BLOCK, s_len)
    nb = s_len // qb
    key_pos = jnp.arange(s_len)

    def blocks(a):
        return jnp.moveaxis(a.reshape(bsz, nb, qb, *a.shape[2:]), 1, 0)

    def one_block(args):
        qn_b, qr_b, t0 = args
        s = (jnp.einsum('bqhd,bshd->bhqs', qn_b, k_nope, preferred_element_type=F32)
             + jnp.einsum('bqhd,bsd->bhqs', qr_b, kr, preferred_element_type=F32)) * ATT_SCALE_D
        s = jnp.where(key_pos[None, :] <= (t0 + jnp.arange(qb))[:, None], s, -jnp.inf)
        p = jax.nn.softmax(s, axis=-1)
        return jnp.einsum('bhqs,bshd->bqhd', p.astype(v.dtype), v)

    out = lax.map(one_block, (blocks(qn), blocks(qr), jnp.arange(nb) * qb))
    return jnp.moveaxis(out, 0, 1).reshape(bsz, s_len, H_D, DV_D)


def mla_sample(qn, qr, ckv, kr, w_uk, w_uv, cache_ckv, cache_krope, page_table):
    t = qn.shape[1]
    q_lat = jnp.einsum('bthd,rhd->bthr', qn.astype(F32), w_uk.astype(F32))
    qrf = qr.astype(F32)

    def scores(c, r):
        return (jnp.einsum('bthr,bsr->bths', q_lat, c.astype(F32))
                + jnp.einsum('bthd,bsd->bths', qrf, r.astype(F32))) * ATT_SCALE_D

    causal = jnp.tril(jnp.ones((t, t), bool))
    s_new = jnp.where(causal[:, None, :], scores(ckv, kr), -jnp.inf)
    m = s_new.max(-1)
    p = jnp.exp(s_new - m[..., None])
    carry0 = (m, p.sum(-1), jnp.einsum('bths,bsr->bthr', p, ckv.astype(F32)))

    def step(carry, pt):
        m, l, acc = carry
        c = cache_ckv[pt]
        s = scores(c, cache_krope[pt])
        m2 = jnp.maximum(m, s.max(-1))
        corr = jnp.exp(m - m2)
        p = jnp.exp(s - m2[..., None])
        return (m2, l * corr + p.sum(-1), acc * corr[..., None] + jnp.einsum('bths,bsr->bthr', p, c.astype(F32))), None

    (m, l, acc), _ = lax.scan(step, carry0, page_table.T)
    return jnp.einsum('bthr,rhd->bthd', acc / l[..., None], w_uv.astype(F32))


def mixer_cd(x, conv_buf, h0, pos0, latent_attn, w_in_cd, conv_w_c, conv_b_c, w_rg_a, b_rg_a,
             w_rg_x, b_rg_x, lambda_c, kv_norm_g_d, w_out_cd):
    bsz, t, _ = x.shape
    h = x @ w_in_cd
    xb, yb, q_d, ckv, kr = _split(h, CD_SPLITS)
    o_c, conv_new, h_new = rglru_branch(xb, yb, conv_buf, h0, conv_w_c, conv_b_c, w_rg_a, b_rg_a,
                                        w_rg_x, b_rg_x, lambda_c)
    q_d = q_d.reshape(bsz, t, H_D, DN_D + DR_D)
    pos = pos0 + jnp.arange(t)
    qn, qr = q_d[..., :DN_D], rope(q_d[..., DN_D:], pos)
    kr = rope(kr, pos)
    ckv = rms_norm(ckv, kv_norm_g_d).astype(x.dtype)
    o_d = latent_attn(qn, qr, ckv, kr)
    o = jnp.concatenate([o_c.astype(x.dtype), o_d.reshape(bsz, t, -1).astype(x.dtype)], -1)
    return o @ w_out_cd, conv_new, h_new, ckv, kr


def setup_inputs(seed: int = 0) -> dict:
    key = jax.random.key(seed)
    keys = jax.random.split(key, 48)
    counter = [0]

    def nxt():
        k = keys[counter[0]]
        counter[0] += 1
        return k

    def nrm(shape, scale=1.0):
        return jax.random.normal(nxt(), shape, F32) * scale

    n_pages = PAST_LEN // PAGE_SIZE
    n_pool = (DEC_BATCH * n_pages * 5) // 4
    perm = jax.random.permutation(nxt(), n_pool)
    page_table = perm[:DEC_BATCH * n_pages].reshape(DEC_BATCH, n_pages).astype(jnp.int32)

    dt = jnp.exp(jax.random.uniform(nxt(), (H_A,), F32, np.log(1e-3), np.log(1e-1)))
    a_pow = jax.random.uniform(nxt(), (D_C,), F32, 0.9, 0.999)
    sig = a_pow ** (1.0 / RG_C)

    return {
        "x_prompt": nrm((BATCH, SEQ, D_MODEL)),
        "x_sample": nrm((DEC_BATCH, DEC_SEQ, D_MODEL)),
        "state_a_conv": nrm((DEC_BATCH, CONV_W - 1, A_QKV)),
        "state_a_rec": nrm((DEC_BATCH, H_A, DK_A, DV_A), 0.1),
        "cache_b_k": nrm((n_pool, PAGE_SIZE, KVH_B, HD_B)),
        "cache_b_v": nrm((n_pool, PAGE_SIZE, KVH_B, HD_B)),
        "cache_b_kidx": nrm((n_pool, PAGE_SIZE, DI_B)),
        "state_c_conv": nrm((DEC_BATCH, CONV_W - 1, D_C)),
        "state_c_h": nrm((DEC_BATCH, D_C), 0.5),
        "cache_d_ckv": nrm((n_pool, PAGE_SIZE, R_KV)),
        "cache_d_krope": nrm((n_pool, PAGE_SIZE, DR_D)),
        "page_table": page_table,
        "ln_g": 1.0 + nrm((DEPTH, 3, D_MODEL), 0.02),
        "ln_b": nrm((DEPTH, 3, D_MODEL), 0.02),
        "ffn_w_gate": nrm((DEPTH, 2, D_MODEL, D_FF), D_MODEL ** -0.5),
        "ffn_w_up": nrm((DEPTH, 2, D_MODEL, D_FF), D_MODEL ** -0.5),
        "ffn_w_down": nrm((DEPTH, 2, D_FF, D_MODEL), D_FF ** -0.5 * DN_BETA),
        "w_in_ab": nrm((D_MODEL, sum(AB_SPLITS)), D_MODEL ** -0.5),
        "conv_w_a": nrm((CONV_W, A_QKV), CONV_W ** -0.5),
        "a_log_a": jnp.log(jax.random.uniform(nxt(), (H_A,), F32, 1.0, 16.0)),
        "dt_bias_a": dt + jnp.log(-jnp.expm1(-dt)),
        "norm_g_a": 1.0 + nrm((DV_A,), 0.02),
        "w_out_ab": nrm((MIX_AB, D_MODEL), MIX_AB ** -0.5 * DN_BETA),
        "w_in_cd": nrm((D_MODEL, sum(CD_SPLITS)), D_MODEL ** -0.5),
        "conv_w_c": nrm((CONV_W, D_C), CONV_W ** -0.5),
        "conv_b_c": nrm((D_C,), 0.02),
        "w_rg_a": nrm((NB_C, BW_C, BW_C), BW_C ** -0.5),
        "b_rg_a": nrm((D_C,), 0.02),
        "w_rg_x": nrm((NB_C, BW_C, BW_C), BW_C ** -0.5),
        "b_rg_x": nrm((D_C,), 0.02),
        "lambda_c": jnp.log(sig) - jnp.log1p(-sig),
        "kv_norm_g_d": 1.0 + nrm((R_KV,), 0.02),
        "w_uk_d": nrm((R_KV, H_D, DN_D), R_KV ** -0.5),
        "w_uv_d": nrm((R_KV, H_D, DV_D), R_KV ** -0.5),
        "w_out_cd": nrm((MIX_CD, D_MODEL), MIX_CD ** -0.5 * DN_BETA),
    }


def reference(x_prompt, x_sample, state_a_conv, state_a_rec, cache_b_k, cache_b_v, cache_b_kidx,
              state_c_conv, state_c_h, cache_d_ckv, cache_d_krope, page_table,
              ln_g, ln_b, ffn_w_gate, ffn_w_up, ffn_w_down,
              w_in_ab, conv_w_a, a_log_a, dt_bias_a, norm_g_a, w_out_ab,
              w_in_cd, conv_w_c, conv_b_c, w_rg_a, b_rg_a, w_rg_x, b_rg_x, lambda_c,
              kv_norm_g_d, w_uk_d, w_uv_d, w_out_cd):
    bp, tp = x_prompt.shape[:2]
    past = page_table.shape[1] * PAGE_SIZE
    yp, ys = x_prompt, x_sample
    for layer in range(DEPTH):
        yp = post_norm(yp, 0.5 * swiglu(yp, ffn_w_gate[layer, 0], ffn_w_up[layer, 0], ffn_w_down[layer, 0]),
                       ln_g[layer, 0], ln_b[layer, 0])
        ys = post_norm(ys, 0.5 * swiglu(ys, ffn_w_gate[layer, 0], ffn_w_up[layer, 0], ffn_w_down[layer, 0]),
                       ln_g[layer, 0], ln_b[layer, 0])
        if layer % 2 == 0:
            mp, a_conv_p, a_rec_p, kb_p, vb_p, kib_p = mixer_ab(
                yp, jnp.zeros((bp, CONV_W - 1, A_QKV), yp.dtype), jnp.zeros((bp, H_A, DK_A, DV_A), F32),
                dsa_prompt, w_in_ab, conv_w_a, a_log_a, dt_bias_a, norm_g_a, w_out_ab)
            attn_s = functools.partial(dsa_sample, cache_k=cache_b_k, cache_v=cache_b_v,
                                       cache_kidx=cache_b_kidx, page_table=page_table)
            ms, a_conv_s, a_rec_s, b_k_s, b_v_s, b_kidx_s = mixer_ab(
                ys, state_a_conv, state_a_rec, attn_s, w_in_ab, conv_w_a, a_log_a, dt_bias_a, norm_g_a, w_out_ab)
            b_k_p = kb_p.reshape(bp, tp // PAGE_SIZE, PAGE_SIZE, KVH_B, HD_B)
            b_v_p = vb_p.reshape(bp, tp // PAGE_SIZE, PAGE_SIZE, KVH_B, HD_B)
            b_kidx_p = kib_p.reshape(bp, tp // PAGE_SIZE, PAGE_SIZE, DI_B)
        else:
            attn_p = functools.partial(mla_prompt, w_uk=w_uk_d, w_uv=w_uv_d)
            mp, c_conv_p, c_h_p, ckv_p, kr_p = mixer_cd(
                yp, jnp.zeros((bp, CONV_W - 1, D_C), yp.dtype), jnp.zeros((bp, D_C), F32), 0, attn_p,
                w_in_cd, conv_w_c, conv_b_c, w_rg_a, b_rg_a, w_rg_x, b_rg_x, lambda_c, kv_norm_g_d, w_out_cd)
            attn_s = functools.partial(mla_sample, w_uk=w_uk_d, w_uv=w_uv_d, cache_ckv=cache_d_ckv,
                                       cache_krope=cache_d_krope, page_table=page_table)
            ms, c_conv_s, c_h_s, d_ckv_s, d_krope_s = mixer_cd(
                ys, state_c_conv, state_c_h, past, attn_s,
                w_in_cd, conv_w_c, conv_b_c, w_rg_a, b_rg_a, w_rg_x, b_rg_x, lambda_c, kv_norm_g_d, w_out_cd)
            d_ckv_p = ckv_p.reshape(bp, tp // PAGE_SIZE, PAGE_SIZE, R_KV)
            d_krope_p = kr_p.reshape(bp, tp // PAGE_SIZE, PAGE_SIZE, DR_D)
        yp = post_norm(yp, mp, ln_g[layer, 1], ln_b[layer, 1])
        ys = post_norm(ys, ms, ln_g[layer, 1], ln_b[layer, 1])
        yp = post_norm(yp, 0.5 * swiglu(yp, ffn_w_gate[layer, 1], ffn_w_up[layer, 1], ffn_w_down[layer, 1]),
                       ln_g[layer, 2], ln_b[layer, 2])
        ys = post_norm(ys, 0.5 * swiglu(ys, ffn_w_gate[layer, 1], ffn_w_up[layer, 1], ffn_w_down[layer, 1]),
                       ln_g[layer, 2], ln_b[layer, 2])
    return (yp, ys,
            a_conv_p, a_conv_s, a_rec_p, a_rec_s,
            b_k_p, b_k_s, b_v_p, b_v_s, b_kidx_p, b_kidx_s,
            c_conv_p, c_conv_s, c_h_p, c_h_s,
            d_ckv_p, d_ckv_s, d_krope_p, d_krope_s)
```

```python
import functools

import jax
import jax.numpy as jnp
import numpy as np
from jax import lax
from jax.experimental import pallas as pl
from jax.experimental.pallas import tpu as pltpu

D_MODEL = 2048
DEPTH = 2
PAGE_SIZE = 128
H_A = 8
DK_A = 128
DV_A = 128
CONV_W = 4
GDN_CHUNK = 64
A_QKV = 2 * H_A * DK_A + H_A * DV_A
H_B = 8
KVH_B = 2
HD_B = 128
HI_B = 16
DI_B = 64
TOPK_MAX = 256
D_C = 1024
NB_C = 8
BW_C = D_C // NB_C
RG_C = 8.0
H_D = 8
DN_D = 128
DR_D = 64
DV_D = 128
R_KV = 256
ROPE_THETA = 10000.0
D_FF = 5632
Q_BLOCK = 128
LN_EPS = 1e-5
RMS_EPS = 1e-6
DN_ALPHA = (2 * DEPTH) ** 0.25
IDX_SCALE = (HI_B * DI_B) ** -0.5
ATT_SCALE_B = HD_B ** -0.5
ATT_SCALE_D = (DN_D + DR_D) ** -0.5
F32 = jnp.float32
BF16 = jnp.bfloat16

VMEM_LIMIT_BYTES = 56 * 1024 * 1024
LANE = 128


def _compiler_params(semantics):
    return pltpu.CompilerParams(dimension_semantics=semantics, vmem_limit_bytes=VMEM_LIMIT_BYTES)


def _layer_norm_rows(y, g, b):
    mu = jnp.mean(y, axis=-1, keepdims=True)
    d = y - mu
    var = jnp.mean(d * d, axis=-1, keepdims=True)
    return d * lax.rsqrt(var + LN_EPS) * g + b


FFN_TM = 512
FFN_TF = 512


def _ffn_ln_body(x_ref, wg_ref, wu_ref, wd_ref, g_ref, b_ref, o_ref, xb_ref, acc_ref):
    j = pl.program_id(1)

    @pl.when(j == 0)
    def _():
        xb_ref[...] = x_ref[...].astype(BF16)
        acc_ref[...] = jnp.zeros_like(acc_ref)

    xb = xb_ref[...]
    hg = jnp.dot(xb, wg_ref[...], preferred_element_type=F32)
    hu = jnp.dot(xb, wu_ref[...], preferred_element_type=F32)
    h = (hg * jax.nn.sigmoid(hg)) * hu
    acc_ref[...] += jnp.dot(h.astype(BF16), wd_ref[...], preferred_element_type=F32)

    @pl.when(j == pl.num_programs(1) - 1)
    def _():
        y = DN_ALPHA * x_ref[...] + 0.5 * acc_ref[...]
        o_ref[...] = _layer_norm_rows(y, g_ref[...], b_ref[...])


def ffn_ln(x, wg, wu, wd, g, b):
    m, d = x.shape
    f = wg.shape[1]
    assert m % FFN_TM == 0 and f % FFN_TF == 0
    return pl.pallas_call(
        _ffn_ln_body,
        grid=(m // FFN_TM, f // FFN_TF),
        in_specs=[
            pl.BlockSpec((FFN_TM, d), lambda i, j: (i, 0)),
            pl.BlockSpec((d, FFN_TF), lambda i, j: (0, j)),
            pl.BlockSpec((d, FFN_TF), lambda i, j: (0, j)),
            pl.BlockSpec((FFN_TF, d), lambda i, j: (j, 0)),
            pl.BlockSpec((1, d), lambda i, j: (0, 0)),
            pl.BlockSpec((1, d), lambda i, j: (0, 0)),
        ],
        out_specs=pl.BlockSpec((FFN_TM, d), lambda i, j: (i, 0)),
        out_shape=jax.ShapeDtypeStruct((m, d), F32),
        scratch_shapes=[pltpu.VMEM((FFN_TM, d), BF16), pltpu.VMEM((FFN_TM, d), F32)],
        compiler_params=_compiler_params(("parallel", "arbitrary")),
        name="ffn_ln",
    )(x, wg, wu, wd, g, b)


MM_TM = 512


def _matmul_body(x_ref, w_ref, o_ref, xb_ref):
    @pl.when(pl.program_id(1) == 0)
    def _():
        xb_ref[...] = x_ref[...].astype(BF16)

    o_ref[...] = jnp.dot(xb_ref[...], w_ref[...], preferred_element_type=F32)


def matmul(x, w, tn):
    m, k = x.shape
    n = w.shape[1]
    assert m % MM_TM == 0 and n % tn == 0
    return pl.pallas_call(
        _matmul_body,
        grid=(m // MM_TM, n // tn),
        in_specs=[
            pl.BlockSpec((MM_TM, k), lambda i, j: (i, 0)),
            pl.BlockSpec((k, tn), lambda i, j: (0, j)),
        ],
        out_specs=pl.BlockSpec((MM_TM, tn), lambda i, j: (i, j)),
        out_shape=jax.ShapeDtypeStruct((m, n), F32),
        scratch_shapes=[pltpu.VMEM((MM_TM, k), BF16)],
        compiler_params=_compiler_params(("parallel", "arbitrary")),
        name="matmul",
    )(x, w)


PROJ_TM = 256


def _proj_ln_body(x_ref, o_in_ref, w_ref, g_ref, b_ref, o_ref):
    f = jnp.dot(o_in_ref[...].astype(BF16), w_ref[...], preferred_element_type=F32)
    y = DN_ALPHA * x_ref[...] + f
    o_ref[...] = _layer_norm_rows(y, g_ref[...], b_ref[...])


def proj_ln(x, o, w, g, b):
    m, d = x.shape
    k = o.shape[1]
    assert m % PROJ_TM == 0
    return pl.pallas_call(
        _proj_ln_body,
        grid=(m // PROJ_TM,),
        in_specs=[
            pl.BlockSpec((PROJ_TM, d), lambda i: (i, 0)),
            pl.BlockSpec((PROJ_TM, k), lambda i: (i, 0)),
            pl.BlockSpec((k, d), lambda i: (0, 0)),
            pl.BlockSpec((1, d), lambda i: (0, 0)),
            pl.BlockSpec((1, d), lambda i: (0, 0)),
        ],
        out_specs=pl.BlockSpec((PROJ_TM, d), lambda i: (i, 0)),
        out_shape=jax.ShapeDtypeStruct((m, d), F32),
        compiler_params=_compiler_params(("parallel",)),
        name="proj_ln",
    )(x, o, w, g, b)


def _l2_norm(x):
    return x * lax.rsqrt(jnp.sum(x * x, -1, keepdims=True) + RMS_EPS)


def _rms_norm(x, g):
    return x * lax.rsqrt(jnp.mean(x * x, -1, keepdims=True) + RMS_EPS) * g


def _causal_conv(x, buf, w):
    t = x.shape[1]
    xp = jnp.concatenate([buf.astype(x.dtype), x], axis=1)
    y = sum(xp[:, j:j + t] * w[j] for j in range(CONV_W))
    return y, xp[:, t:]


def _rope(x, pos):
    half = DR_D // 2
    inv = ROPE_THETA ** (-jnp.arange(half, dtype=F32) / half)
    ang = pos.astype(F32)[:, None] * inv
    ang = ang.reshape(ang.shape[0], *([1] * (x.ndim - 3)), half)
    cos, sin = jnp.cos(ang), jnp.sin(ang)
    x1, x2 = x[..., :half], x[..., half:]
    return jnp.concatenate([x1 * cos - x2 * sin, x2 * cos + x1 * sin], -1)


def _chunk(a, n, c):
    a = jnp.pad(a, [(0, 0), (0, n * c - a.shape[1])] + [(0, 0)] * (a.ndim - 2))
    a = a.reshape(a.shape[0], n, c, *a.shape[2:])
    return jnp.swapaxes(jnp.swapaxes(a, 0, 1), 2, 3)


def _gated_delta(q, k, v, g, beta, s0):
    bsz, t = q.shape[:2]
    c = min(GDN_CHUNK, t)
    n = -(-t // c)
    qc = _chunk(q * DK_A ** -0.5, n, c)
    kc, vc = _chunk(k, n, c), _chunk(v, n, c)
    gc, bc = _chunk(g, n, c), _chunk(beta, n, c)
    gcum = jnp.cumsum(gc, axis=-1)
    incl = jnp.tril(jnp.ones((c, c), bool))
    strict = jnp.tril(jnp.ones((c, c), bool), -1)
    decay = jnp.exp(jnp.where(incl, gcum[..., :, None] - gcum[..., None, :], -jnp.inf))
    kb = kc * bc[..., None]
    lmat = jnp.where(strict, jnp.einsum('nbhid,nbhjd->nbhij', kb, kc) * decay, 0.0)
    eye = jnp.eye(c, dtype=F32)
    tmat = lax.linalg.triangular_solve(eye + lmat, jnp.broadcast_to(eye, lmat.shape),
                                       left_side=True, lower=True, unit_diagonal=True)
    u = tmat @ (vc * bc[..., None])
    w = tmat @ (kb * jnp.exp(gcum)[..., None])
    qk = jnp.einsum('nbhid,nbhjd->nbhij', qc, kc) * decay
    qg = qc * jnp.exp(gcum)[..., None]
    kd = kc * jnp.exp(gcum[..., -1:] - gcum)[..., None]
    glast = jnp.exp(gcum[..., -1])

    def step(s, xs):
        u_i, w_i, qk_i, qg_i, kd_i, gl_i = xs
        v_new = u_i - jnp.einsum('bhcd,bhde->bhce', w_i, s)
        o = jnp.einsum('bhcd,bhde->bhce', qg_i, s) + jnp.einsum('bhij,bhje->bhie', qk_i, v_new)
        s = s * gl_i[..., None, None] + jnp.einsum('bhcd,bhce->bhde', kd_i, v_new)
        return s, o

    s, o = lax.scan(step, s0, (u, w, qk, qg, kd, glast))
    o = jnp.swapaxes(jnp.swapaxes(o, 2, 3), 0, 1).reshape(bsz, n * c, H_A, DV_A)[:, :t]
    return o, s


def _index_scores(qi, keys, wi):
    s = jax.nn.relu(jnp.einsum('bthd,bsd->bths', qi, keys, preferred_element_type=F32))
    return jnp.einsum('bth,bths->bts', wi, s)


def _gathered_attention(q, kg, vg, valid):
    bsz, t = q.shape[:2]
    qg = q.reshape(bsz, t, KVH_B, H_B // KVH_B, HD_B)
    s = jnp.einsum('btkgd,btnkd->btkgn', qg, kg, preferred_element_type=F32) * ATT_SCALE_B
    s = jnp.where(valid[:, :, None, None, :], s, -jnp.inf)
    p = jax.nn.softmax(s, axis=-1)
    o = jnp.einsum('btkgn,btnkd->btkgd', p, vg)
    return o.reshape(bsz, t, H_B, HD_B)


def _dsa_prompt(q, k, v, qi, ki, wi):
    bsz, s_len = q.shape[:2]
    ksel = min(TOPK_MAX, s_len // 4)
    qb = min(Q_BLOCK, s_len)
    nb = s_len // qb
    key_pos = jnp.arange(s_len)
    take = jax.vmap(lambda a, i: a[i])

    def blocks(a):
        return jnp.moveaxis(a.reshape(bsz, nb, qb, *a.shape[2:]), 1, 0)

    def one_block(args):
        q_blk, qi_blk, wi_blk, t0 = args
        sc = _index_scores(qi_blk, ki, wi_blk)
        q_pos = t0 + jnp.arange(qb)
        sc = jnp.where(key_pos[None, None, :] <= q_pos[None, :, None], sc, -jnp.inf)
        val, idx = lax.top_k(sc, ksel)
        return _gathered_attention(q_blk, take(k, idx), take(v, idx), jnp.isfinite(val))

    out = lax.map(one_block, (blocks(q), blocks(qi), blocks(wi), jnp.arange(nb) * qb))
    return jnp.moveaxis(out, 0, 1).reshape(bsz, s_len, H_B, HD_B)


def _dsa_sample(q, k, v, qi, ki, wi, cache_k, cache_v, cache_kidx, page_table):
    bsz, t = q.shape[:2]
    past = page_table.shape[1] * PAGE_SIZE
    ksel = min(TOPK_MAX, (past + t) // 4)
    sc_past = lax.map(lambda pt: _index_scores(qi, cache_kidx[pt], wi), page_table.T)
    sc_past = jnp.moveaxis(sc_past, 0, 2).reshape(bsz, t, past)
    causal = jnp.tril(jnp.ones((t, t), bool))
    sc_new = jnp.where(causal, _index_scores(qi, ki, wi), -jnp.inf)
    val, idx = lax.top_k(jnp.concatenate([sc_past, sc_new], -1), ksel)
    bi = jnp.arange(bsz)[:, None, None]
    is_past = (idx < past)[..., None, None]
    pidx = jnp.minimum(idx, past - 1)
    phys = page_table[bi, pidx // PAGE_SIZE]
    off = pidx % PAGE_SIZE
    nidx = jnp.clip(idx - past, 0, t - 1)
    kg = jnp.where(is_past, cache_k[phys, off], k[bi, nidx])
    vg = jnp.where(is_past, cache_v[phys, off], v[bi, nidx])
    return _gathered_attention(q, kg, vg, jnp.isfinite(val))


def _mixer_ab_core(hm, hs, conv_buf, rec_state, sparse_attn, conv_w_a, a_log_a, dt_bias_a, norm_g_a):
    bsz, t, _ = hm.shape
    qkv_pre = hm[..., :A_QKV]
    z = hm[..., A_QKV:A_QKV + 1024]
    q_b = hm[..., 4096:5120]
    qi_b = hm[..., 5120:6144]
    k_b = hm[..., 6144:6400]
    v_b = hm[..., 6400:6656]
    ki_b = hs[..., :DI_B]
    b_gate = hs[..., 64:72]
    a_gate = hs[..., 72:80]
    wi_b = hs[..., 80:96]
    qkv, conv_new = _causal_conv(qkv_pre, conv_buf, conv_w_a)
    qkv = jax.nn.silu(qkv)
    qa = _l2_norm(qkv[..., :1024].reshape(bsz, t, H_A, DK_A))
    ka = _l2_norm(qkv[..., 1024:2048].reshape(bsz, t, H_A, DK_A))
    va = qkv[..., 2048:].reshape(bsz, t, H_A, DV_A)
    beta = jax.nn.sigmoid(b_gate)
    g = -jnp.exp(a_log_a) * jax.nn.softplus(a_gate + dt_bias_a)
    o_a, s_new = _gated_delta(qa, ka, va, g, beta, rec_state)
    o_a = _rms_norm(o_a, norm_g_a) * jax.nn.silu(z.reshape(bsz, t, H_A, DV_A))
    q_b = q_b.reshape(bsz, t, H_B, HD_B)
    k_b = k_b.reshape(bsz, t, KVH_B, HD_B)
    v_b = v_b.reshape(bsz, t, KVH_B, HD_B)
    qi_b = qi_b.reshape(bsz, t, HI_B, DI_B)
    o_b = sparse_attn(q_b, k_b, v_b, qi_b, ki_b, wi_b * IDX_SCALE)
    o = jnp.concatenate([o_a.reshape(bsz, t, -1), o_b.reshape(bsz, t, -1)], -1)
    return o, conv_new, s_new, k_b, v_b, ki_b


def _rglru_branch(xb, yb, conv_buf, h0, conv_w_c, conv_b_c, w_rg_a, b_rg_a, w_rg_x, b_rg_x, lambda_c):
    bsz, t, _ = xb.shape
    xc, conv_new = _causal_conv(xb, conv_buf, conv_w_c)
    xc = xc + conv_b_c
    xr = xc.reshape(bsz, t, NB_C, BW_C)
    r = jax.nn.sigmoid(jnp.einsum('btnd,nde->btne', xr, w_rg_a).reshape(bsz, t, D_C) + b_rg_a)
    i = jax.nn.sigmoid(jnp.einsum('btnd,nde->btne', xr, w_rg_x).reshape(bsz, t, D_C) + b_rg_x)
    log_a = -RG_C * r * jax.nn.softplus(-lambda_c)
    a = jnp.exp(log_a)
    bterm = jnp.sqrt(-jnp.expm1(2.0 * log_a)) * (i * xc)
    bterm = bterm.at[:, 0].add(a[:, 0] * h0)

    def comb(lhs, rhs):
        return lhs[0] * rhs[0], rhs[0] * lhs[1] + rhs[1]

    _, h = lax.associative_scan(comb, (a, bterm), axis=1)
    return h * jax.nn.gelu(yb), conv_new, h[:, -1]


def _mla_prompt(qn, qr, ckv, kr, w_uk, w_uv):
    bsz, s_len = qn.shape[:2]
    k_nope = jnp.einsum('bsr,rhd->bshd', ckv, w_uk)
    v = jnp.einsum('bsr,rhd->bshd', ckv, w_uv)
    qb = min(Q_BLOCK, s_len)
    nb = s_len // qb
    key_pos = jnp.arange(s_len)

    def blocks(a):
        return jnp.moveaxis(a.reshape(bsz, nb, qb, *a.shape[2:]), 1, 0)

    def one_block(args):
        qn_b, qr_b, t0 = args
        s = (jnp.einsum('bqhd,bshd->bhqs', qn_b, k_nope, preferred_element_type=F32)
             + jnp.einsum('bqhd,bsd->bhqs', qr_b, kr, preferred_element_type=F32)) * ATT_SCALE_D
        s = jnp.where(key_pos[None, :] <= (t0 + jnp.arange(qb))[:, None], s, -jnp.inf)
        p = jax.nn.softmax(s, axis=-1)
        return jnp.einsum('bhqs,bshd->bqhd', p, v)

    out = lax.map(one_block, (blocks(qn), blocks(qr), jnp.arange(nb) * qb))
    return jnp.moveaxis(out, 0, 1).reshape(bsz, s_len, H_D, DV_D)


def _mla_sample(qn, qr, ckv, kr, w_uk, w_uv, cache_ckv, cache_krope, page_table):
    t = qn.shape[1]
    q_lat = jnp.einsum('bthd,rhd->bthr', qn, w_uk)

    def scores(c, r):
        return (jnp.einsum('bthr,bsr->bths', q_lat, c)
                + jnp.einsum('bthd,bsd->bths', qr, r)) * ATT_SCALE_D

    causal = jnp.tril(jnp.ones((t, t), bool))
    s_new = jnp.where(causal[:, None, :], scores(ckv, kr), -jnp.inf)
    m = s_new.max(-1)
    p = jnp.exp(s_new - m[..., None])
    carry0 = (m, p.sum(-1), jnp.einsum('bths,bsr->bthr', p, ckv))

    def step(carry, pt):
        m, l, acc = carry
        c = cache_ckv[pt]
        s = scores(c, cache_krope[pt])
        m2 = jnp.maximum(m, s.max(-1))
        corr = jnp.exp(m - m2)
        p = jnp.exp(s - m2[..., None])
        return (m2, l * corr + p.sum(-1), acc * corr[..., None] + jnp.einsum('bths,bsr->bthr', p, c)), None

    (m, l, acc), _ = lax.scan(step, carry0, page_table.T)
    return jnp.einsum('bthr,rhd->bthd', acc / l[..., None], w_uv)


def _mixer_cd_core(hm, hs, conv_buf, h0, pos0, latent_attn, conv_w_c, conv_b_c, w_rg_a, b_rg_a,
                   w_rg_x, b_rg_x, lambda_c, kv_norm_g_d):
    bsz, t, _ = hm.shape
    xb = hm[..., :D_C]
    yb = hm[..., D_C:2 * D_C]
    q_d = hm[..., 2 * D_C:2 * D_C + H_D * (DN_D + DR_D)]
    ckv = hm[..., 3584:3840]
    kr = hs[..., :DR_D]
    o_c, conv_new, h_new = _rglru_branch(xb, yb, conv_buf, h0, conv_w_c, conv_b_c, w_rg_a, b_rg_a,
                                         w_rg_x, b_rg_x, lambda_c)
    q_d = q_d.reshape(bsz, t, H_D, DN_D + DR_D)
    pos = pos0 + jnp.arange(t)
    qn, qr = q_d[..., :DN_D], _rope(q_d[..., DN_D:], pos)
    kr = _rope(kr, pos)
    ckv = _rms_norm(ckv, kv_norm_g_d)
    o_d = latent_attn(qn, qr, ckv, kr)
    o = jnp.concatenate([o_c, o_d.reshape(bsz, t, -1)], -1)
    return o, conv_new, h_new, ckv, kr


def _pad_cols(w, n):
    return jnp.pad(w, ((0, 0), (0, n - w.shape[1])))


def kernel(x_prompt, x_sample, state_a_conv, state_a_rec, cache_b_k, cache_b_v, cache_b_kidx, state_c_conv, state_c_h, cache_d_ckv, cache_d_krope, page_table, ln_g, ln_b, ffn_w_gate, ffn_w_up, ffn_w_down, w_in_ab, conv_w_a, a_log_a, dt_bias_a, norm_g_a, w_out_ab, w_in_cd, conv_w_c, conv_b_c, w_rg_a, b_rg_a, w_rg_x, b_rg_x, lambda_c, kv_norm_g_d, w_uk_d, w_uv_d, w_out_cd):
    bp, tp = x_prompt.shape[:2]
    bs, ts = x_sample.shape[:2]
    mp = bp * tp
    past = page_table.shape[1] * PAGE_SIZE
    n_pg = tp // PAGE_SIZE

    x = jnp.concatenate([x_prompt.reshape(mp, D_MODEL), x_sample.reshape(bs * ts, D_MODEL)], 0)

    wg = ffn_w_gate.astype(BF16)
    wu = ffn_w_up.astype(BF16)
    wd = ffn_w_down.astype(BF16)

    ab = np.cumsum((0, A_QKV, H_A * DV_A, H_A, H_A, H_B * HD_B, KVH_B * HD_B, KVH_B * HD_B, HI_B * DI_B, DI_B, HI_B))
    col = lambda w, i: w[:, int(ab[i]):int(ab[i + 1])]
    w_ab_main = jnp.concatenate([col(w_in_ab, 0), col(w_in_ab, 1), col(w_in_ab, 4), col(w_in_ab, 7),
                                 col(w_in_ab, 5), col(w_in_ab, 6)], 1).astype(BF16)
    w_ab_small = _pad_cols(jnp.concatenate([col(w_in_ab, 8), col(w_in_ab, 2), col(w_in_ab, 3), col(w_in_ab, 9)], 1),
                           LANE).astype(BF16)
    w_cd_main = w_in_cd[:, :3840].astype(BF16)
    w_cd_small = _pad_cols(w_in_cd[:, 3840:], LANE).astype(BF16)
    w_out_ab_b = w_out_ab.astype(BF16)
    w_out_cd_b = w_out_cd.astype(BF16)

    def ffn(x, layer, half, ln_idx):
        return ffn_ln(x, wg[layer, half], wu[layer, half], wd[layer, half],
                      ln_g[layer, ln_idx][None], ln_b[layer, ln_idx][None])

    x = ffn(x, 0, 0, 0)
    hm = matmul(x, w_ab_main, 512)
    hs = matmul(x, w_ab_small, LANE)
    o_p, a_conv_p, a_rec_p, kb_p, vb_p, kib_p = _mixer_ab_core(
        hm[:mp].reshape(bp, tp, -1), hs[:mp].reshape(bp, tp, -1),
        jnp.zeros((bp, CONV_W - 1, A_QKV), F32), jnp.zeros((bp, H_A, DK_A, DV_A), F32),
        _dsa_prompt, conv_w_a, a_log_a, dt_bias_a, norm_g_a)
    attn_s = functools.partial(_dsa_sample, cache_k=cache_b_k, cache_v=cache_b_v,
                               cache_kidx=cache_b_kidx, page_table=page_table)
    o_s, a_conv_s, a_rec_s, b_k_s, b_v_s, b_kidx_s = _mixer_ab_core(
        hm[mp:].reshape(bs, ts, -1), hs[mp:].reshape(bs, ts, -1), state_a_conv, state_a_rec,
        attn_s, conv_w_a, a_log_a, dt_bias_a, norm_g_a)
    b_k_p = kb_p.reshape(bp, n_pg, PAGE_SIZE, KVH_B, HD_B)
    b_v_p = vb_p.reshape(bp, n_pg, PAGE_SIZE, KVH_B, HD_B)
    b_kidx_p = kib_p.reshape(bp, n_pg, PAGE_SIZE, DI_B)
    o = jnp.concatenate([o_p.reshape(mp, -1), o_s.reshape(bs * ts, -1)], 0)
    x = proj_ln(x, o, w_out_ab_b, ln_g[0, 1][None], ln_b[0, 1][None])
    x = ffn(x, 0, 1, 2)

    x = ffn(x, 1, 0, 0)
    hm = matmul(x, w_cd_main, 768)
    hs = matmul(x, w_cd_small, LANE)
    attn_p = functools.partial(_mla_prompt, w_uk=w_uk_d, w_uv=w_uv_d)
    o_p, c_conv_p, c_h_p, ckv_p, kr_p = _mixer_cd_core(
        hm[:mp].reshape(bp, tp, -1), hs[:mp].reshape(bp, tp, -1),
        jnp.zeros((bp, CONV_W - 1, D_C), F32), jnp.zeros((bp, D_C), F32), 0, attn_p,
        conv_w_c, conv_b_c, w_rg_a, b_rg_a, w_rg_x, b_rg_x, lambda_c, kv_norm_g_d)
    attn_s = functools.partial(_mla_sample, w_uk=w_uk_d, w_uv=w_uv_d, cache_ckv=cache_d_ckv,
                               cache_krope=cache_d_krope, page_table=page_table)
    o_s, c_conv_s, c_h_s, d_ckv_s, d_krope_s = _mixer_cd_core(
        hm[mp:].reshape(bs, ts, -1), hs[mp:].reshape(bs, ts, -1), state_c_conv, state_c_h, past, attn_s,
        conv_w_c, conv_b_c, w_rg_a, b_rg_a, w_rg_x, b_rg_x, lambda_c, kv_norm_g_d)
    d_ckv_p = ckv_p.reshape(bp, n_pg, PAGE_SIZE, R_KV)
    d_krope_p = kr_p.reshape(bp, n_pg, PAGE_SIZE, DR_D)
    o = jnp.concatenate([o_p.reshape(mp, -1), o_s.reshape(bs * ts, -1)], 0)
    x = proj_ln(x, o, w_out_cd_b, ln_g[1, 1][None], ln_b[1, 1][None])
    x = ffn(x, 1, 1, 2)

    yp = x[:mp].reshape(bp, tp, D_MODEL)
    ys = x[mp:].reshape(bs, ts, D_MODEL)
    return (yp, ys,
            a_conv_p, a_conv_s, a_rec_p, a_rec_s,
            b_k_p, b_k_s, b_v_p, b_v_s, b_kidx_p, b_kidx_s,
            c_conv_p, c_conv_s, c_h_p, c_h_s,
            d_ckv_p, d_ckv_s, d_krope_p, d_krope_s)
```

```python
import functools

import jax
import jax.numpy as jnp
import numpy as np
from jax import lax
from jax.experimental import pallas as pl
from jax.experimental.pallas import tpu as pltpu

D_MODEL = 2048
DEPTH = 2
PAGE_SIZE = 128
H_A = 8
DK_A = 128
DV_A = 128
CONV_W = 4
GDN_CHUNK = 64
A_QKV = 2 * H_A * DK_A + H_A * DV_A
H_B = 8
KVH_B = 2
HD_B = 128
HI_B = 16
DI_B = 64
TOPK_MAX = 256
D_C = 1024
NB_C = 8
BW_C = D_C // NB_C
RG_C = 8.0
H_D = 8
DN_D = 128
DR_D = 64
DV_D = 128
R_KV = 256
ROPE_THETA = 10000.0
D_FF = 5632
Q_BLOCK = 128
LN_EPS = 1e-5
RMS_EPS = 1e-6
DN_ALPHA = (2 * DEPTH) ** 0.25
IDX_SCALE = (HI_B * DI_B) ** -0.5
ATT_SCALE_B = HD_B ** -0.5
ATT_SCALE_D = (DN_D + DR_D) ** -0.5
F32 = jnp.float32
BF16 = jnp.bfloat16

VMEM_LIMIT_BYTES = 56 * 1024 * 1024
LANE = 128


def _compiler_params(semantics):
    return pltpu.CompilerParams(dimension_semantics=semantics, vmem_limit_bytes=VMEM_LIMIT_BYTES)


def _layer_norm_rows(y, g, b):
    mu = jnp.mean(y, axis=-1, keepdims=True)
    d = y - mu
    var = jnp.mean(d * d, axis=-1, keepdims=True)
    return d * lax.rsqrt(var + LN_EPS) * g + b


FFN_TM = 512
FFN_TF = 512


def _ffn_ln_body(x_ref, wg_ref, wu_ref, wd_ref, g_ref, b_ref, o_ref, xb_ref, acc_ref):
    j = pl.program_id(1)

    @pl.when(j == 0)
    def _():
        xb_ref[...] = x_ref[...].astype(BF16)
        acc_ref[...] = jnp.zeros_like(acc_ref)

    xb = xb_ref[...]
    hg = jnp.dot(xb, wg_ref[...], preferred_element_type=F32)
    hu = jnp.dot(xb, wu_ref[...], preferred_element_type=F32)
    h = (hg * jax.nn.sigmoid(hg)) * hu
    acc_ref[...] += jnp.dot(h.astype(BF16), wd_ref[...], preferred_element_type=F32)

    @pl.when(j == pl.num_programs(1) - 1)
    def _():
        y = DN_ALPHA * x_ref[...] + 0.5 * acc_ref[...]
        o_ref[...] = _layer_norm_rows(y, g_ref[...], b_ref[...])


def ffn_ln(x, wg, wu, wd, g, b):
    m, d = x.shape
    f = wg.shape[1]
    assert m % FFN_TM == 0 and f % FFN_TF == 0
    return pl.pallas_call(
        _ffn_ln_body,
        grid=(m // FFN_TM, f // FFN_TF),
        in_specs=[
            pl.BlockSpec((FFN_TM, d), lambda i, j: (i, 0)),
            pl.BlockSpec((d, FFN_TF), lambda i, j: (0, j)),
            pl.BlockSpec((d, FFN_TF), lambda i, j: (0, j)),
            pl.BlockSpec((FFN_TF, d), lambda i, j: (j, 0)),
            pl.BlockSpec((1, d), lambda i, j: (0, 0)),
            pl.BlockSpec((1, d), lambda i, j: (0, 0)),
        ],
        out_specs=pl.BlockSpec((FFN_TM, d), lambda i, j: (i, 0)),
        out_shape=jax.ShapeDtypeStruct((m, d), F32),
        scratch_shapes=[pltpu.VMEM((FFN_TM, d), BF16), pltpu.VMEM((FFN_TM, d), F32)],
        compiler_params=_compiler_params(("parallel", "arbitrary")),
        name="ffn_ln",
    )(x, wg, wu, wd, g, b)


MM_TM = 512


def _matmul_body(x_ref, w_ref, o_ref, xb_ref):
    @pl.when(pl.program_id(1) == 0)
    def _():
        xb_ref[...] = x_ref[...].astype(BF16)

    o_ref[...] = jnp.dot(xb_ref[...], w_ref[...], preferred_element_type=F32).astype(o_ref.dtype)


def matmul(x, w, tn, out_dtype=F32):
    m, k = x.shape
    n = w.shape[1]
    assert m % MM_TM == 0 and n % tn == 0
    return pl.pallas_call(
        _matmul_body,
        grid=(m // MM_TM, n // tn),
        in_specs=[
            pl.BlockSpec((MM_TM, k), lambda i, j: (i, 0)),
            pl.BlockSpec((k, tn), lambda i, j: (0, j)),
        ],
        out_specs=pl.BlockSpec((MM_TM, tn), lambda i, j: (i, j)),
        out_shape=jax.ShapeDtypeStruct((m, n), out_dtype),
        scratch_shapes=[pltpu.VMEM((MM_TM, k), BF16)],
        compiler_params=_compiler_params(("parallel", "arbitrary")),
        name="matmul",
    )(x, w)


DSA_TQ = 128
DSA_KC = 512
MASKED = -1e30
INT32_MIN = -2 ** 31
AB_Q_COL, AB_QI_COL, AB_K_COL, AB_V_COL = 4096, 5120, 6144, 6400
AB_WI_LANE = 80


def _order_key(x):
    bits = lax.bitcast_convert_type(jnp.where(x == 0.0, 0.0, x), jnp.int32)
    return jnp.where(bits < 0, bits ^ 0x7FFFFFFF, bits)


def _dsa_prompt_body(ksel, q_ref, qi_ref, hsq_ref, k_ref, v_ref, hsk_ref, o_ref,
                     kb_ref, vb_ref, kk_ref, keyt_ref, bias_ref, m_ref, l_ref, acc_ref):
    i = pl.program_id(1)
    tq, kc = DSA_TQ, DSA_KC
    n_all = kk_ref.shape[0] // kc
    n_grp = H_B // KVH_B
    contract_last = (((1,), (1,)), ((), ()))

    @pl.when(i == 0)
    def _():
        def cast_chunk(c, carry):
            rows = pl.ds(pl.multiple_of(c * kc, kc), kc)
            kf = k_ref[rows, :]
            vf = v_ref[rows, :]
            for g in range(KVH_B):
                kb_ref[g, rows, :] = kf[:, g * HD_B:(g + 1) * HD_B].astype(BF16)
                vb_ref[g, rows, :] = vf[:, g * HD_B:(g + 1) * HD_B].astype(BF16)
            hs = hsk_ref[rows, :]
            lane = lax.broadcasted_iota(jnp.int32, hs.shape, 1)
            kk_ref[rows, :] = jnp.where(lane < DI_B, hs, pltpu.roll(hs, DI_B, 1)).astype(BF16)
            return carry

        lax.fori_loop(0, n_all, cast_chunk, 0)

    nk = (i * tq + tq + kc - 1) // kc

    qi = qi_ref[...]
    lane = lax.broadcasted_iota(jnp.int32, (tq, 2 * DI_B), 1)
    rhs = []
    for j in range(HI_B // 2):
        pair = qi[:, j * 2 * DI_B:(j + 1) * 2 * DI_B]
        rhs.append(jnp.concatenate([jnp.where(lane < DI_B, pair, 0.0),
                                    jnp.where(lane >= DI_B, pair, 0.0)], 0).astype(BF16))
    wi_t = jnp.transpose(hsq_ref[...])[AB_WI_LANE:AB_WI_LANE + HI_B, :] * IDX_SCALE
    t_pos = i * tq + lax.broadcasted_iota(jnp.int32, (kc, tq), 1)

    def score_chunk(c, carry):
        kk = kk_ref[pl.ds(pl.multiple_of(c * kc, kc), kc), :]
        sc = jnp.zeros((kc, tq), F32)
        for j in range(HI_B // 2):
            o = lax.dot_general(kk, rhs[j], contract_last, preferred_element_type=F32)
            sc = sc + wi_t[2 * j:2 * j + 1, :] * jnp.maximum(o[:, :tq], 0.0)
            sc = sc + wi_t[2 * j + 1:2 * j + 2, :] * jnp.maximum(o[:, tq:], 0.0)
        s_pos = c * kc + lax.broadcasted_iota(jnp.int32, (kc, tq), 0)
        keyt_ref[c] = jnp.where(s_pos <= t_pos, _order_key(sc), INT32_MIN)
        return carry

    lax.fori_loop(0, nk, score_chunk, 0)

    def count_ge(trial):
        def body(c, cnt):
            ge = (keyt_ref[c] >= trial).astype(jnp.int32)
            return cnt + jnp.sum(ge.reshape(kc // 8, 8, tq), axis=0)

        cnt = lax.fori_loop(0, nk, body, jnp.zeros((8, tq), jnp.int32))
        return jnp.sum(cnt, axis=0, keepdims=True)

    def bit_pass(it, prefix):
        trial = prefix | lax.shift_left(jnp.int32(1), 31 - it)
        return jnp.where(count_ge(trial ^ INT32_MIN) >= ksel, trial, prefix)

    prefix = lax.fori_loop(0, 32, bit_pass, jnp.zeros((1, tq), jnp.int32))
    thr = jnp.maximum(prefix ^ INT32_MIN, INT32_MIN + 1)

    def bias_chunk(c, carry):
        bias_ref[c] = jnp.transpose(jnp.where(keyt_ref[c] >= thr, 0.0, MASKED))
        return carry

    lax.fori_loop(0, nk, bias_chunk, 0)

    q = q_ref[...]
    qg = [jnp.concatenate([q[:, (g * n_grp + hh) * HD_B:(g * n_grp + hh + 1) * HD_B] for hh in range(n_grp)],
                          0).astype(BF16) for g in range(KVH_B)]
    m_ref[...] = jnp.full(m_ref.shape, MASKED, F32)
    l_ref[...] = jnp.zeros(l_ref.shape, F32)
    acc_ref[...] = jnp.zeros(acc_ref.shape, F32)

    def attn_chunk(c, carry):
        rows = pl.ds(pl.multiple_of(c * kc, kc), kc)
        bias = bias_ref[c]
        for g in range(KVH_B):
            s = lax.dot_general(qg[g], kb_ref[g, rows, :], contract_last, preferred_element_type=F32) * ATT_SCALE_B
            s = (s.reshape(n_grp, tq, kc) + bias[None]).reshape(n_grp * tq, kc)
            m_old = m_ref[g]
            m_new = jnp.maximum(m_old, jnp.max(s, axis=1, keepdims=True))
            p = jnp.exp(s - m_new)
            corr = jnp.exp(m_old - m_new)
            l_ref[g] = l_ref[g] * corr + jnp.sum(p, axis=1, keepdims=True)
            acc_ref[g] = acc_ref[g] * corr + jnp.dot(p.astype(BF16), vb_ref[g, rows, :], preferred_element_type=F32)
            m_ref[g] = m_new
        return carry

    lax.fori_loop(0, nk, attn_chunk, 0)

    for g in range(KVH_B):
        o = acc_ref[g] / l_ref[g]
        for hh in range(n_grp):
            o_ref[:, (g * n_grp + hh) * HD_B:(g * n_grp + hh + 1) * HD_B] = o[hh * tq:(hh + 1) * tq, :]


def dsa_prompt(hm, hs, bsz, t):
    tq, kc = DSA_TQ, DSA_KC
    assert t % kc == 0 and kc % tq == 0
    nq = t // tq
    ksel = min(TOPK_MAX, t // 4)
    qw = H_B * HD_B
    kw = KVH_B * HD_B
    n_grp = H_B // KVH_B
    return pl.pallas_call(
        functools.partial(_dsa_prompt_body, ksel),
        grid=(bsz, nq),
        in_specs=[
            pl.BlockSpec((tq, qw), lambda b, i: (b * nq + i, AB_Q_COL // qw)),
            pl.BlockSpec((tq, HI_B * DI_B), lambda b, i: (b * nq + i, AB_QI_COL // (HI_B * DI_B))),
            pl.BlockSpec((tq, LANE), lambda b, i: (b * nq + i, 0)),
            pl.BlockSpec((t, kw), lambda b, i: (b, AB_K_COL // kw)),
            pl.BlockSpec((t, kw), lambda b, i: (b, AB_V_COL // kw)),
            pl.BlockSpec((t, LANE), lambda b, i: (b, 0)),
        ],
        out_specs=pl.BlockSpec((tq, qw), lambda b, i: (b * nq + i, 0)),
        out_shape=jax.ShapeDtypeStruct((bsz * t, qw), F32),
        scratch_shapes=[
            pltpu.VMEM((KVH_B, t, HD_B), BF16),
            pltpu.VMEM((KVH_B, t, HD_B), BF16),
            pltpu.VMEM((t, LANE), BF16),
            pltpu.VMEM((t // kc, kc, tq), jnp.int32),
            pltpu.VMEM((t // kc, tq, kc), F32),
            pltpu.VMEM((KVH_B, n_grp * tq, 1), F32),
            pltpu.VMEM((KVH_B, n_grp * tq, 1), F32),
            pltpu.VMEM((KVH_B, n_grp * tq, HD_B), F32),
        ],
        compiler_params=_compiler_params(("parallel", "arbitrary")),
        name="dsa_prompt",
    )(hm, hm, hs, hm, hm, hs)


MLA_TQ = 512


def _mla_prompt_body(qn_ref, qr_ref, kn_ref, kr_ref, v_ref, o_ref, m_ref, l_ref, acc_ref):
    i = pl.program_id(2)
    tq = MLA_TQ
    contract_last = (((1,), (1,)), ((), ()))
    qn = qn_ref[...].astype(BF16)
    qr = qr_ref[...].astype(BF16)
    m_ref[...] = jnp.full(m_ref.shape, MASKED, F32)
    l_ref[...] = jnp.zeros(l_ref.shape, F32)
    acc_ref[...] = jnp.zeros(acc_ref.shape, F32)

    def chunk(c, diagonal):
        rows = pl.ds(pl.multiple_of(c * tq, tq), tq)
        s = (lax.dot_general(qn, kn_ref[rows, :], contract_last, preferred_element_type=F32)
             + lax.dot_general(qr, kr_ref[rows, :], contract_last, preferred_element_type=F32)) * ATT_SCALE_D
        if diagonal:
            causal = (lax.broadcasted_iota(jnp.int32, (tq, tq), 1) <= lax.broadcasted_iota(jnp.int32, (tq, tq), 0))
            s = jnp.where(causal, s, MASKED)
        m_old = m_ref[...]
        m_new = jnp.maximum(m_old, jnp.max(s, axis=1, keepdims=True))
        p = jnp.exp(s - m_new)
        corr = jnp.exp(m_old - m_new)
        l_ref[...] = l_ref[...] * corr + jnp.sum(p, axis=1, keepdims=True)
        acc_ref[...] = acc_ref[...] * corr + jnp.dot(p.astype(BF16), v_ref[rows, :], preferred_element_type=F32)
        m_ref[...] = m_new

    def below_diagonal(c, carry):
        chunk(c, False)
        return carry

    lax.fori_loop(0, i, below_diagonal, 0)
    chunk(i, True)
    o_ref[...] = acc_ref[...] / l_ref[...]


def mla_prompt(qn, qr, kn, kr, v, bsz, t):
    tq = MLA_TQ
    assert t % tq == 0
    nq = t // tq
    return pl.pallas_call(
        _mla_prompt_body,
        grid=(bsz, H_D, nq),
        in_specs=[
            pl.BlockSpec((tq, DN_D), lambda b, h, i: (b * nq + i, h)),
            pl.BlockSpec((None, tq, DR_D), lambda b, h, i: (h, b * nq + i, 0)),
            pl.BlockSpec((t, DN_D), lambda b, h, i: (b, h)),
            pl.BlockSpec((t, DR_D), lambda b, h, i: (b, 0)),
            pl.BlockSpec((t, DV_D), lambda b, h, i: (b, h)),
        ],
        out_specs=pl.BlockSpec((tq, DV_D), lambda b, h, i: (b * nq + i, h)),
        out_shape=jax.ShapeDtypeStruct((bsz * t, H_D * DV_D), F32),
        scratch_shapes=[
            pltpu.VMEM((tq, 1), F32),
            pltpu.VMEM((tq, 1), F32),
            pltpu.VMEM((tq, DV_D), F32),
        ],
        compiler_params=_compiler_params(("parallel", "parallel", "arbitrary")),
        name="mla_prompt",
    )(qn, qr, kn, kr, v)


PROJ_TM = 256


def _proj_ln_body(x_ref, o_in_ref, w_ref, g_ref, b_ref, o_ref):
    f = jnp.dot(o_in_ref[...].astype(BF16), w_ref[...], preferred_element_type=F32)
    y = DN_ALPHA * x_ref[...] + f
    o_ref[...] = _layer_norm_rows(y, g_ref[...], b_ref[...])


def proj_ln(x, o, w, g, b):
    m, d = x.shape
    k = o.shape[1]
    assert m % PROJ_TM == 0
    return pl.pallas_call(
        _proj_ln_body,
        grid=(m // PROJ_TM,),
        in_specs=[
            pl.BlockSpec((PROJ_TM, d), lambda i: (i, 0)),
            pl.BlockSpec((PROJ_TM, k), lambda i: (i, 0)),
            pl.BlockSpec((k, d), lambda i: (0, 0)),
            pl.BlockSpec((1, d), lambda i: (0, 0)),
            pl.BlockSpec((1, d), lambda i: (0, 0)),
        ],
        out_specs=pl.BlockSpec((PROJ_TM, d), lambda i: (i, 0)),
        out_shape=jax.ShapeDtypeStruct((m, d), F32),
        compiler_params=_compiler_params(("parallel",)),
        name="proj_ln",
    )(x, o, w, g, b)


def _l2_norm(x):
    return x * lax.rsqrt(jnp.sum(x * x, -1, keepdims=True) + RMS_EPS)


def _rms_norm(x, g):
    return x * lax.rsqrt(jnp.mean(x * x, -1, keepdims=True) + RMS_EPS) * g


def _causal_conv(x, buf, w):
    t = x.shape[1]
    xp = jnp.concatenate([buf.astype(x.dtype), x], axis=1)
    y = sum(xp[:, j:j + t] * w[j] for j in range(CONV_W))
    return y, xp[:, t:]


def _rope(x, pos):
    half = DR_D // 2
    inv = ROPE_THETA ** (-jnp.arange(half, dtype=F32) / half)
    ang = pos.astype(F32)[:, None] * inv
    ang = ang.reshape(ang.shape[0], *([1] * (x.ndim - 3)), half)
    cos, sin = jnp.cos(ang), jnp.sin(ang)
    x1, x2 = x[..., :half], x[..., half:]
    return jnp.concatenate([x1 * cos - x2 * sin, x2 * cos + x1 * sin], -1)


def _chunk(a, n, c):
    a = jnp.pad(a, [(0, 0), (0, n * c - a.shape[1])] + [(0, 0)] * (a.ndim - 2))
    a = a.reshape(a.shape[0], n, c, *a.shape[2:])
    return jnp.swapaxes(jnp.swapaxes(a, 0, 1), 2, 3)


def _gated_delta(q, k, v, g, beta, s0):
    bsz, t = q.shape[:2]
    c = min(GDN_CHUNK, t)
    n = -(-t // c)
    qc = _chunk(q * DK_A ** -0.5, n, c)
    kc, vc = _chunk(k, n, c), _chunk(v, n, c)
    gc, bc = _chunk(g, n, c), _chunk(beta, n, c)
    gcum = jnp.cumsum(gc, axis=-1)
    incl = jnp.tril(jnp.ones((c, c), bool))
    strict = jnp.tril(jnp.ones((c, c), bool), -1)
    decay = jnp.exp(jnp.where(incl, gcum[..., :, None] - gcum[..., None, :], -jnp.inf))
    kb = kc * bc[..., None]
    lmat = jnp.where(strict, jnp.einsum('nbhid,nbhjd->nbhij', kb, kc) * decay, 0.0)
    eye = jnp.eye(c, dtype=F32)
    tmat = lax.linalg.triangular_solve(eye + lmat, jnp.broadcast_to(eye, lmat.shape),
                                       left_side=True, lower=True, unit_diagonal=True)
    u = tmat @ (vc * bc[..., None])
    w = tmat @ (kb * jnp.exp(gcum)[..., None])
    qk = jnp.einsum('nbhid,nbhjd->nbhij', qc, kc) * decay
    qg = qc * jnp.exp(gcum)[..., None]
    kd = kc * jnp.exp(gcum[..., -1:] - gcum)[..., None]
    glast = jnp.exp(gcum[..., -1])

    def step(s, xs):
        u_i, w_i, qk_i, qg_i, kd_i, gl_i = xs
        v_new = u_i - jnp.einsum('bhcd,bhde->bhce', w_i, s)
        o = jnp.einsum('bhcd,bhde->bhce', qg_i, s) + jnp.einsum('bhij,bhje->bhie', qk_i, v_new)
        s = s * gl_i[..., None, None] + jnp.einsum('bhcd,bhce->bhde', kd_i, v_new)
        return s, o

    s, o = lax.scan(step, s0, (u, w, qk, qg, kd, glast))
    o = jnp.swapaxes(jnp.swapaxes(o, 2, 3), 0, 1).reshape(bsz, n * c, H_A, DV_A)[:, :t]
    return o, s


def _index_scores(qi, keys, wi):
    s = jax.nn.relu(jnp.einsum('bthd,bsd->bths', qi, keys, preferred_element_type=F32))
    return jnp.einsum('bth,bths->bts', wi, s)


def _gathered_attention(q, kg, vg, valid):
    bsz, t = q.shape[:2]
    qg = q.reshape(bsz, t, KVH_B, H_B // KVH_B, HD_B)
    s = jnp.einsum('btkgd,btnkd->btkgn', qg, kg, preferred_element_type=F32) * ATT_SCALE_B
    s = jnp.where(valid[:, :, None, None, :], s, -jnp.inf)
    p = jax.nn.softmax(s, axis=-1)
    o = jnp.einsum('btkgn,btnkd->btkgd', p, vg)
    return o.reshape(bsz, t, H_B, HD_B)


def _dsa_sample(q, k, v, qi, ki, wi, cache_k, cache_v, cache_kidx, page_table):
    bsz, t = q.shape[:2]
    past = page_table.shape[1] * PAGE_SIZE
    ksel = min(TOPK_MAX, (past + t) // 4)
    sc_past = lax.map(lambda pt: _index_scores(qi, cache_kidx[pt], wi), page_table.T)
    sc_past = jnp.moveaxis(sc_past, 0, 2).reshape(bsz, t, past)
    causal = jnp.tril(jnp.ones((t, t), bool))
    sc_new = jnp.where(causal, _index_scores(qi, ki, wi), -jnp.inf)
    val, idx = lax.top_k(jnp.concatenate([sc_past, sc_new], -1), ksel)
    bi = jnp.arange(bsz)[:, None, None]
    is_past = (idx < past)[..., None, None]
    pidx = jnp.minimum(idx, past - 1)
    phys = page_table[bi, pidx // PAGE_SIZE]
    off = pidx % PAGE_SIZE
    nidx = jnp.clip(idx - past, 0, t - 1)
    kg = jnp.where(is_past, cache_k[phys, off], k[bi, nidx])
    vg = jnp.where(is_past, cache_v[phys, off], v[bi, nidx])
    return _gathered_attention(q, kg, vg, jnp.isfinite(val))


def _mixer_ab_core(hm, hs, conv_buf, rec_state, sparse_attn, conv_w_a, a_log_a, dt_bias_a, norm_g_a):
    bsz, t, _ = hm.shape
    qkv_pre = hm[..., :A_QKV]
    z = hm[..., A_QKV:A_QKV + 1024]
    q_b = hm[..., 4096:5120]
    qi_b = hm[..., 5120:6144]
    k_b = hm[..., 6144:6400]
    v_b = hm[..., 6400:6656]
    ki_b = hs[..., :DI_B]
    b_gate = hs[..., 64:72]
    a_gate = hs[..., 72:80]
    wi_b = hs[..., 80:96]
    qkv, conv_new = _causal_conv(qkv_pre, conv_buf, conv_w_a)
    qkv = jax.nn.silu(qkv)
    qa = _l2_norm(qkv[..., :1024].reshape(bsz, t, H_A, DK_A))
    ka = _l2_norm(qkv[..., 1024:2048].reshape(bsz, t, H_A, DK_A))
    va = qkv[..., 2048:].reshape(bsz, t, H_A, DV_A)
    beta = jax.nn.sigmoid(b_gate)
    g = -jnp.exp(a_log_a) * jax.nn.softplus(a_gate + dt_bias_a)
    o_a, s_new = _gated_delta(qa, ka, va, g, beta, rec_state)
    o_a = _rms_norm(o_a, norm_g_a) * jax.nn.silu(z.reshape(bsz, t, H_A, DV_A))
    q_b = q_b.reshape(bsz, t, H_B, HD_B)
    k_b = k_b.reshape(bsz, t, KVH_B, HD_B)
    v_b = v_b.reshape(bsz, t, KVH_B, HD_B)
    qi_b = qi_b.reshape(bsz, t, HI_B, DI_B)
    o_b = sparse_attn(q_b, k_b, v_b, qi_b, ki_b, wi_b * IDX_SCALE)
    o = jnp.concatenate([o_a.reshape(bsz, t, -1), o_b.reshape(bsz, t, -1)], -1)
    return o, conv_new, s_new, k_b, v_b, ki_b


def _rglru_branch(xb, yb, conv_buf, h0, conv_w_c, conv_b_c, w_rg_a, b_rg_a, w_rg_x, b_rg_x, lambda_c):
    bsz, t, _ = xb.shape
    xc, conv_new = _causal_conv(xb, conv_buf, conv_w_c)
    xc = xc + conv_b_c
    xr = xc.reshape(bsz, t, NB_C, BW_C)
    r = jax.nn.sigmoid(jnp.einsum('btnd,nde->btne', xr, w_rg_a).reshape(bsz, t, D_C) + b_rg_a)
    i = jax.nn.sigmoid(jnp.einsum('btnd,nde->btne', xr, w_rg_x).reshape(bsz, t, D_C) + b_rg_x)
    log_a = -RG_C * r * jax.nn.softplus(-lambda_c)
    a = jnp.exp(log_a)
    bterm = jnp.sqrt(-jnp.expm1(2.0 * log_a)) * (i * xc)
    bterm = bterm.at[:, 0].add(a[:, 0] * h0)

    def comb(lhs, rhs):
        return lhs[0] * rhs[0], rhs[0] * lhs[1] + rhs[1]

    _, h = lax.associative_scan(comb, (a, bterm), axis=1)
    return h * jax.nn.gelu(yb), conv_new, h[:, -1]


def _mla_prompt(qn, qr, ckv, kr, w_uk, w_uv):
    bsz, s_len = qn.shape[:2]
    m = bsz * s_len
    ckv2 = ckv.reshape(m, R_KV)
    k_nope = matmul(ckv2, w_uk.reshape(R_KV, H_D * DN_D).astype(BF16), H_D * DN_D, BF16)
    v = matmul(ckv2, w_uv.reshape(R_KV, H_D * DV_D).astype(BF16), H_D * DV_D, BF16)
    qr_hm = jnp.moveaxis(qr.reshape(m, H_D, DR_D), 1, 0)
    out = mla_prompt(qn.reshape(m, H_D * DN_D), qr_hm, k_nope, kr.reshape(m, DR_D).astype(BF16), v, bsz, s_len)
    return out.reshape(bsz, s_len, H_D, DV_D)


def _mla_sample(qn, qr, ckv, kr, w_uk, w_uv, cache_ckv, cache_krope, page_table):
    t = qn.shape[1]
    q_lat = jnp.einsum('bthd,rhd->bthr', qn, w_uk)

    def scores(c, r):
        return (jnp.einsum('bthr,bsr->bths', q_lat, c)
                + jnp.einsum('bthd,bsd->bths', qr, r)) * ATT_SCALE_D

    causal = jnp.tril(jnp.ones((t, t), bool))
    s_new = jnp.where(causal[:, None, :], scores(ckv, kr), -jnp.inf)
    m = s_new.max(-1)
    p = jnp.exp(s_new - m[..., None])
    carry0 = (m, p.sum(-1), jnp.einsum('bths,bsr->bthr', p, ckv))

    def step(carry, pt):
        m, l, acc = carry
        c = cache_ckv[pt]
        s = scores(c, cache_krope[pt])
        m2 = jnp.maximum(m, s.max(-1))
        corr = jnp.exp(m - m2)
        p = jnp.exp(s - m2[..., None])
        return (m2, l * corr + p.sum(-1), acc * corr[..., None] + jnp.einsum('bths,bsr->bthr', p, c)), None

    (m, l, acc), _ = lax.scan(step, carry0, page_table.T)
    return jnp.einsum('bthr,rhd->bthd', acc / l[..., None], w_uv)


def _mixer_cd_core(hm, hs, conv_buf, h0, pos0, latent_attn, conv_w_c, conv_b_c, w_rg_a, b_rg_a,
                   w_rg_x, b_rg_x, lambda_c, kv_norm_g_d):
    bsz, t, _ = hm.shape
    xb = hm[..., :D_C]
    yb = hm[..., D_C:2 * D_C]
    qn = hm[..., 2048:3072].reshape(bsz, t, H_D, DN_D)
    qr = hm[..., 3072:3584].reshape(bsz, t, H_D, DR_D)
    ckv = hm[..., 3584:3840]
    kr = hs[..., :DR_D]
    o_c, conv_new, h_new = _rglru_branch(xb, yb, conv_buf, h0, conv_w_c, conv_b_c, w_rg_a, b_rg_a,
                                         w_rg_x, b_rg_x, lambda_c)
    pos = pos0 + jnp.arange(t)
    qr = _rope(qr, pos)
    kr = _rope(kr, pos)
    ckv = _rms_norm(ckv, kv_norm_g_d)
    o_d = latent_attn(qn, qr, ckv, kr)
    o = jnp.concatenate([o_c, o_d.reshape(bsz, t, -1)], -1)
    return o, conv_new, h_new, ckv, kr


def _pad_cols(w, n):
    return jnp.pad(w, ((0, 0), (0, n - w.shape[1])))


def kernel(x_prompt, x_sample, state_a_conv, state_a_rec, cache_b_k, cache_b_v, cache_b_kidx, state_c_conv, state_c_h, cache_d_ckv, cache_d_krope, page_table, ln_g, ln_b, ffn_w_gate, ffn_w_up, ffn_w_down, w_in_ab, conv_w_a, a_log_a, dt_bias_a, norm_g_a, w_out_ab, w_in_cd, conv_w_c, conv_b_c, w_rg_a, b_rg_a, w_rg_x, b_rg_x, lambda_c, kv_norm_g_d, w_uk_d, w_uv_d, w_out_cd):
    bp, tp = x_prompt.shape[:2]
    bs, ts = x_sample.shape[:2]
    mp = bp * tp
    past = page_table.shape[1] * PAGE_SIZE
    n_pg = tp // PAGE_SIZE

    x = jnp.concatenate([x_prompt.reshape(mp, D_MODEL), x_sample.reshape(bs * ts, D_MODEL)], 0)

    wg = ffn_w_gate.astype(BF16)
    wu = ffn_w_up.astype(BF16)
    wd = ffn_w_down.astype(BF16)

    ab = np.cumsum((0, A_QKV, H_A * DV_A, H_A, H_A, H_B * HD_B, KVH_B * HD_B, KVH_B * HD_B, HI_B * DI_B, DI_B, HI_B))
    col = lambda w, i: w[:, int(ab[i]):int(ab[i + 1])]
    w_ab_main = jnp.concatenate([col(w_in_ab, 0), col(w_in_ab, 1), col(w_in_ab, 4), col(w_in_ab, 7),
                                 col(w_in_ab, 5), col(w_in_ab, 6)], 1).astype(BF16)
    w_ab_small = _pad_cols(jnp.concatenate([col(w_in_ab, 8), col(w_in_ab, 2), col(w_in_ab, 3), col(w_in_ab, 9)], 1),
                           LANE).astype(BF16)
    w_q_d = w_in_cd[:, 2 * D_C:2 * D_C + H_D * (DN_D + DR_D)].reshape(D_MODEL, H_D, DN_D + DR_D)
    w_cd_main = jnp.concatenate([w_in_cd[:, :2 * D_C],
                                 w_q_d[..., :DN_D].reshape(D_MODEL, H_D * DN_D),
                                 w_q_d[..., DN_D:].reshape(D_MODEL, H_D * DR_D),
                                 w_in_cd[:, 3584:3840]], 1).astype(BF16)
    w_cd_small = _pad_cols(w_in_cd[:, 3840:], LANE).astype(BF16)
    w_out_ab_b = w_out_ab.astype(BF16)
    w_out_cd_b = w_out_cd.astype(BF16)

    def ffn(x, layer, half, ln_idx):
        return ffn_ln(x, wg[layer, half], wu[layer, half], wd[layer, half],
                      ln_g[layer, ln_idx][None], ln_b[layer, ln_idx][None])

    x = ffn(x, 0, 0, 0)
    hm = matmul(x, w_ab_main, 512)
    hs = matmul(x, w_ab_small, LANE)
    o_p, a_conv_p, a_rec_p, kb_p, vb_p, kib_p = _mixer_ab_core(
        hm[:mp].reshape(bp, tp, -1), hs[:mp].reshape(bp, tp, -1),
        jnp.zeros((bp, CONV_W - 1, A_QKV), F32), jnp.zeros((bp, H_A, DK_A, DV_A), F32),
        lambda *unused: dsa_prompt(hm, hs, bp, tp), conv_w_a, a_log_a, dt_bias_a, norm_g_a)
    attn_s = functools.partial(_dsa_sample, cache_k=cache_b_k, cache_v=cache_b_v,
                               cache_kidx=cache_b_kidx, page_table=page_table)
    o_s, a_conv_s, a_rec_s, b_k_s, b_v_s, b_kidx_s = _mixer_ab_core(
        hm[mp:].reshape(bs, ts, -1), hs[mp:].reshape(bs, ts, -1), state_a_conv, state_a_rec,
        attn_s, conv_w_a, a_log_a, dt_bias_a, norm_g_a)
    b_k_p = kb_p.reshape(bp, n_pg, PAGE_SIZE, KVH_B, HD_B)
    b_v_p = vb_p.reshape(bp, n_pg, PAGE_SIZE, KVH_B, HD_B)
    b_kidx_p = kib_p.reshape(bp, n_pg, PAGE_SIZE, DI_B)
    o = jnp.concatenate([o_p.reshape(mp, -1), o_s.reshape(bs * ts, -1)], 0)
    x = proj_ln(x, o, w_out_ab_b, ln_g[0, 1][None], ln_b[0, 1][None])
    x = ffn(x, 0, 1, 2)

    x = ffn(x, 1, 0, 0)
    hm = matmul(x, w_cd_main, 768)
    hs = matmul(x, w_cd_small, LANE)
    attn_p = functools.partial(_mla_prompt, w_uk=w_uk_d, w_uv=w_uv_d)
    o_p, c_conv_p, c_h_p, ckv_p, kr_p = _mixer_cd_core(
        hm[:mp].reshape(bp, tp, -1), hs[:mp].reshape(bp, tp, -1),
        jnp.zeros((bp, CONV_W - 1, D_C), F32), jnp.zeros((bp, D_C), F32), 0, attn_p,
        conv_w_c, conv_b_c, w_rg_a, b_rg_a, w_rg_x, b_rg_x, lambda_c, kv_norm_g_d)
    attn_s = functools.partial(_mla_sample, w_uk=w_uk_d, w_uv=w_uv_d, cache_ckv=cache_d_ckv,
                               cache_krope=cache_d_krope, page_table=page_table)
    o_s, c_conv_s, c_h_s, d_ckv_s, d_krope_s = _mixer_cd_core(
        hm[mp:].reshape(bs, ts, -1), hs[mp:].reshape(bs, ts, -1), state_c_conv, state_c_h, past, attn_s,
        conv_w_c, conv_b_c, w_rg_a, b_rg_a, w_rg_x, b_rg_x, lambda_c, kv_norm_g_d)
    d_ckv_p = ckv_p.reshape(bp, n_pg, PAGE_SIZE, R_KV)
    d_krope_p = kr_p.reshape(bp, n_pg, PAGE_SIZE, DR_D)
    o = jnp.concatenate([o_p.reshape(mp, -1), o_s.reshape(bs * ts, -1)], 0)
    x = proj_ln(x, o, w_out_cd_b, ln_g[1, 1][None], ln_b[1, 1][None])
    x = ffn(x, 1, 1, 2)

    yp = x[:mp].reshape(bp, tp, D_MODEL)
    ys = x[mp:].reshape(bs, ts, D_MODEL)
    return (yp, ys,
            a_conv_p, a_conv_s, a_rec_p, a_rec_s,
            b_k_p, b_k_s, b_v_p, b_v_s, b_kidx_p, b_kidx_s,
            c_conv_p, c_conv_s, c_h_p, c_h_s,
            d_ckv_p, d_ckv_s, d_krope_p, d_krope_s)
```

```python
import functools

import jax
import jax.numpy as jnp
import numpy as np
from jax import lax
from jax.experimental import pallas as pl
from jax.experimental.pallas import tpu as pltpu

D_MODEL = 2048
DEPTH = 2
PAGE_SIZE = 128
H_A = 8
DK_A = 128
DV_A = 128
CONV_W = 4
GDN_CHUNK = 64
A_QKV = 2 * H_A * DK_A + H_A * DV_A
H_B = 8
KVH_B = 2
HD_B = 128
HI_B = 16
DI_B = 64
TOPK_MAX = 256
D_C = 1024
NB_C = 8
BW_C = D_C // NB_C
RG_C = 8.0
H_D = 8
DN_D = 128
DR_D = 64
DV_D = 128
R_KV = 256
ROPE_THETA = 10000.0
D_FF = 5632
Q_BLOCK = 128
LN_EPS = 1e-5
RMS_EPS = 1e-6
DN_ALPHA = (2 * DEPTH) ** 0.25
IDX_SCALE = (HI_B * DI_B) ** -0.5
ATT_SCALE_B = HD_B ** -0.5
ATT_SCALE_D = (DN_D + DR_D) ** -0.5
F32 = jnp.float32
BF16 = jnp.bfloat16

VMEM_LIMIT_BYTES = 56 * 1024 * 1024
LANE = 128


def _compiler_params(semantics):
    return pltpu.CompilerParams(dimension_semantics=semantics, vmem_limit_bytes=VMEM_LIMIT_BYTES)


def _layer_norm_rows(y, g, b):
    mu = jnp.mean(y, axis=-1, keepdims=True)
    d = y - mu
    var = jnp.mean(d * d, axis=-1, keepdims=True)
    return d * lax.rsqrt(var + LN_EPS) * g + b


FFN_TM = 512
FFN_TF = 512


def _ffn_ln_body(x_ref, wg_ref, wu_ref, wd_ref, g_ref, b_ref, o_ref, xb_ref, acc_ref):
    j = pl.program_id(1)

    @pl.when(j == 0)
    def _():
        xb_ref[...] = x_ref[...].astype(BF16)
        acc_ref[...] = jnp.zeros_like(acc_ref)

    xb = xb_ref[...]
    hg = jnp.dot(xb, wg_ref[...], preferred_element_type=F32)
    hu = jnp.dot(xb, wu_ref[...], preferred_element_type=F32)
    h = (hg * jax.nn.sigmoid(hg)) * hu
    acc_ref[...] += jnp.dot(h.astype(BF16), wd_ref[...], preferred_element_type=F32)

    @pl.when(j == pl.num_programs(1) - 1)
    def _():
        y = DN_ALPHA * x_ref[...] + 0.5 * acc_ref[...]
        o_ref[...] = _layer_norm_rows(y, g_ref[...], b_ref[...])


def ffn_ln(x, wg, wu, wd, g, b):
    m, d = x.shape
    f = wg.shape[1]
    assert m % FFN_TM == 0 and f % FFN_TF == 0
    return pl.pallas_call(
        _ffn_ln_body,
        grid=(m // FFN_TM, f // FFN_TF),
        in_specs=[
            pl.BlockSpec((FFN_TM, d), lambda i, j: (i, 0)),
            pl.BlockSpec((d, FFN_TF), lambda i, j: (0, j)),
            pl.BlockSpec((d, FFN_TF), lambda i, j: (0, j)),
            pl.BlockSpec((FFN_TF, d), lambda i, j: (j, 0)),
            pl.BlockSpec((1, d), lambda i, j: (0, 0)),
            pl.BlockSpec((1, d), lambda i, j: (0, 0)),
        ],
        out_specs=pl.BlockSpec((FFN_TM, d), lambda i, j: (i, 0)),
        out_shape=jax.ShapeDtypeStruct((m, d), F32),
        scratch_shapes=[pltpu.VMEM((FFN_TM, d), BF16), pltpu.VMEM((FFN_TM, d), F32)],
        compiler_params=_compiler_params(("parallel", "arbitrary")),
        name="ffn_ln",
    )(x, wg, wu, wd, g, b)


MM_TM = 512


def _matmul_body(x_ref, w_ref, o_ref, xb_ref):
    @pl.when(pl.program_id(1) == 0)
    def _():
        xb_ref[...] = x_ref[...].astype(BF16)

    o_ref[...] = jnp.dot(xb_ref[...], w_ref[...], preferred_element_type=F32).astype(o_ref.dtype)


def matmul(x, w, tn, out_dtype=F32):
    m, k = x.shape
    n = w.shape[1]
    assert m % MM_TM == 0 and n % tn == 0
    return pl.pallas_call(
        _matmul_body,
        grid=(m // MM_TM, n // tn),
        in_specs=[
            pl.BlockSpec((MM_TM, k), lambda i, j: (i, 0)),
            pl.BlockSpec((k, tn), lambda i, j: (0, j)),
        ],
        out_specs=pl.BlockSpec((MM_TM, tn), lambda i, j: (i, j)),
        out_shape=jax.ShapeDtypeStruct((m, n), out_dtype),
        scratch_shapes=[pltpu.VMEM((MM_TM, k), BF16)],
        compiler_params=_compiler_params(("parallel", "arbitrary")),
        name="matmul",
    )(x, w)


DSA_TQ = 128
DSA_KC = 512
MASKED = -1e30
INT32_MIN = -2 ** 31
AB_Q_COL, AB_QI_COL, AB_K_COL, AB_V_COL = 4096, 5120, 6144, 6400
AB_WI_LANE = 80


def _order_key(x):
    bits = lax.bitcast_convert_type(jnp.where(x == 0.0, 0.0, x), jnp.int32)
    return jnp.where(bits < 0, bits ^ 0x7FFFFFFF, bits)


def _dsa_prompt_body(ksel, q_ref, qi_ref, hsq_ref, k_ref, v_ref, hsk_ref, o_ref,
                     kb_ref, vb_ref, kk_ref, keyt_ref, bias_ref, m_ref, l_ref, acc_ref):
    i = pl.program_id(1)
    tq, kc = DSA_TQ, DSA_KC
    n_all = kk_ref.shape[0] // kc
    n_grp = H_B // KVH_B
    contract_last = (((1,), (1,)), ((), ()))

    @pl.when(i == 0)
    def _():
        def cast_chunk(c, carry):
            rows = pl.ds(pl.multiple_of(c * kc, kc), kc)
            kf = k_ref[rows, :]
            vf = v_ref[rows, :]
            for g in range(KVH_B):
                kb_ref[g, rows, :] = kf[:, g * HD_B:(g + 1) * HD_B].astype(BF16)
                vb_ref[g, rows, :] = vf[:, g * HD_B:(g + 1) * HD_B].astype(BF16)
            hs = hsk_ref[rows, :]
            lane = lax.broadcasted_iota(jnp.int32, hs.shape, 1)
            kk_ref[rows, :] = jnp.where(lane < DI_B, hs, pltpu.roll(hs, DI_B, 1)).astype(BF16)
            return carry

        lax.fori_loop(0, n_all, cast_chunk, 0)

    nk = (i * tq + tq + kc - 1) // kc

    qi = qi_ref[...]
    lane = lax.broadcasted_iota(jnp.int32, (tq, 2 * DI_B), 1)
    rhs = []
    for j in range(HI_B // 2):
        pair = qi[:, j * 2 * DI_B:(j + 1) * 2 * DI_B]
        rhs.append(jnp.concatenate([jnp.where(lane < DI_B, pair, 0.0),
                                    jnp.where(lane >= DI_B, pair, 0.0)], 0).astype(BF16))
    wi_t = jnp.transpose(hsq_ref[...])[AB_WI_LANE:AB_WI_LANE + HI_B, :] * IDX_SCALE
    t_pos = i * tq + lax.broadcasted_iota(jnp.int32, (kc, tq), 1)

    def score_chunk(c, carry):
        kk = kk_ref[pl.ds(pl.multiple_of(c * kc, kc), kc), :]
        sc = jnp.zeros((kc, tq), F32)
        for j in range(HI_B // 2):
            o = lax.dot_general(kk, rhs[j], contract_last, preferred_element_type=F32)
            sc = sc + wi_t[2 * j:2 * j + 1, :] * jnp.maximum(o[:, :tq], 0.0)
            sc = sc + wi_t[2 * j + 1:2 * j + 2, :] * jnp.maximum(o[:, tq:], 0.0)
        s_pos = c * kc + lax.broadcasted_iota(jnp.int32, (kc, tq), 0)
        keyt_ref[c] = jnp.where(s_pos <= t_pos, _order_key(sc), INT32_MIN)
        return carry

    lax.fori_loop(0, nk, score_chunk, 0)

    def count_ge(trial):
        def body(c, cnt):
            ge = (keyt_ref[c] >= trial).astype(jnp.int32)
            return cnt + jnp.sum(ge.reshape(kc // 8, 8, tq), axis=0)

        cnt = lax.fori_loop(0, nk, body, jnp.zeros((8, tq), jnp.int32))
        return jnp.sum(cnt, axis=0, keepdims=True)

    def bit_pass(it, prefix):
        trial = prefix | lax.shift_left(jnp.int32(1), 31 - it)
        return jnp.where(count_ge(trial ^ INT32_MIN) >= ksel, trial, prefix)

    prefix = lax.fori_loop(0, 32, bit_pass, jnp.zeros((1, tq), jnp.int32))
    thr = jnp.maximum(prefix ^ INT32_MIN, INT32_MIN + 1)

    def bias_chunk(c, carry):
        bias_ref[c] = jnp.transpose(jnp.where(keyt_ref[c] >= thr, 0.0, MASKED))
        return carry

    lax.fori_loop(0, nk, bias_chunk, 0)

    q = q_ref[...]
    qg = [jnp.concatenate([q[:, (g * n_grp + hh) * HD_B:(g * n_grp + hh + 1) * HD_B] for hh in range(n_grp)],
                          0).astype(BF16) for g in range(KVH_B)]
    m_ref[...] = jnp.full(m_ref.shape, MASKED, F32)
    l_ref[...] = jnp.zeros(l_ref.shape, F32)
    acc_ref[...] = jnp.zeros(acc_ref.shape, F32)

    def attn_chunk(c, carry):
        rows = pl.ds(pl.multiple_of(c * kc, kc), kc)
        bias = bias_ref[c]
        for g in range(KVH_B):
            s = lax.dot_general(qg[g], kb_ref[g, rows, :], contract_last, preferred_element_type=F32) * ATT_SCALE_B
            s = (s.reshape(n_grp, tq, kc) + bias[None]).reshape(n_grp * tq, kc)
            m_old = m_ref[g]
            m_new = jnp.maximum(m_old, jnp.max(s, axis=1, keepdims=True))
            p = jnp.exp(s - m_new)
            corr = jnp.exp(m_old - m_new)
            l_ref[g] = l_ref[g] * corr + jnp.sum(p, axis=1, keepdims=True)
            acc_ref[g] = acc_ref[g] * corr + jnp.dot(p.astype(BF16), vb_ref[g, rows, :], preferred_element_type=F32)
            m_ref[g] = m_new
        return carry

    lax.fori_loop(0, nk, attn_chunk, 0)

    for g in range(KVH_B):
        o = acc_ref[g] / l_ref[g]
        for hh in range(n_grp):
            o_ref[:, (g * n_grp + hh) * HD_B:(g * n_grp + hh + 1) * HD_B] = o[hh * tq:(hh + 1) * tq, :]


def dsa_prompt(hm, hs, bsz, t):
    tq, kc = DSA_TQ, DSA_KC
    assert t % kc == 0 and kc % tq == 0
    nq = t // tq
    ksel = min(TOPK_MAX, t // 4)
    qw = H_B * HD_B
    kw = KVH_B * HD_B
    n_grp = H_B // KVH_B
    return pl.pallas_call(
        functools.partial(_dsa_prompt_body, ksel),
        grid=(bsz, nq),
        in_specs=[
            pl.BlockSpec((tq, qw), lambda b, i: (b * nq + i, AB_Q_COL // qw)),
            pl.BlockSpec((tq, HI_B * DI_B), lambda b, i: (b * nq + i, AB_QI_COL // (HI_B * DI_B))),
            pl.BlockSpec((tq, LANE), lambda b, i: (b * nq + i, 0)),
            pl.BlockSpec((t, kw), lambda b, i: (b, AB_K_COL // kw)),
            pl.BlockSpec((t, kw), lambda b, i: (b, AB_V_COL // kw)),
            pl.BlockSpec((t, LANE), lambda b, i: (b, 0)),
        ],
        out_specs=pl.BlockSpec((tq, qw), lambda b, i: (b * nq + i, 0)),
        out_shape=jax.ShapeDtypeStruct((bsz * t, qw), F32),
        scratch_shapes=[
            pltpu.VMEM((KVH_B, t, HD_B), BF16),
            pltpu.VMEM((KVH_B, t, HD_B), BF16),
            pltpu.VMEM((t, LANE), BF16),
            pltpu.VMEM((t // kc, kc, tq), jnp.int32),
            pltpu.VMEM((t // kc, tq, kc), F32),
            pltpu.VMEM((KVH_B, n_grp * tq, 1), F32),
            pltpu.VMEM((KVH_B, n_grp * tq, 1), F32),
            pltpu.VMEM((KVH_B, n_grp * tq, HD_B), F32),
        ],
        compiler_params=_compiler_params(("parallel", "arbitrary")),
        name="dsa_prompt",
    )(hm, hm, hs, hm, hm, hs)


MLA_TQ = 512


def _mla_prompt_body(qn_ref, qr_ref, kn_ref, kr_ref, v_ref, o_ref, m_ref, l_ref, acc_ref):
    i = pl.program_id(2)
    tq = MLA_TQ
    contract_last = (((1,), (1,)), ((), ()))
    qn = qn_ref[...].astype(BF16)
    qr = qr_ref[...].astype(BF16)
    m_ref[...] = jnp.full(m_ref.shape, MASKED, F32)
    l_ref[...] = jnp.zeros(l_ref.shape, F32)
    acc_ref[...] = jnp.zeros(acc_ref.shape, F32)

    def chunk(c, diagonal):
        rows = pl.ds(pl.multiple_of(c * tq, tq), tq)
        s = (lax.dot_general(qn, kn_ref[rows, :], contract_last, preferred_element_type=F32)
             + lax.dot_general(qr, kr_ref[rows, :], contract_last, preferred_element_type=F32)) * ATT_SCALE_D
        if diagonal:
            causal = (lax.broadcasted_iota(jnp.int32, (tq, tq), 1) <= lax.broadcasted_iota(jnp.int32, (tq, tq), 0))
            s = jnp.where(causal, s, MASKED)
        m_old = m_ref[...]
        m_new = jnp.maximum(m_old, jnp.max(s, axis=1, keepdims=True))
        p = jnp.exp(s - m_new)
        corr = jnp.exp(m_old - m_new)
        l_ref[...] = l_ref[...] * corr + jnp.sum(p, axis=1, keepdims=True)
        acc_ref[...] = acc_ref[...] * corr + jnp.dot(p.astype(BF16), v_ref[rows, :], preferred_element_type=F32)
        m_ref[...] = m_new

    def below_diagonal(c, carry):
        chunk(c, False)
        return carry

    lax.fori_loop(0, i, below_diagonal, 0)
    chunk(i, True)
    o_ref[...] = acc_ref[...] / l_ref[...]


def mla_prompt(qn, qr, kn, kr, v, bsz, t):
    tq = MLA_TQ
    assert t % tq == 0
    nq = t // tq
    return pl.pallas_call(
        _mla_prompt_body,
        grid=(bsz, H_D, nq),
        in_specs=[
            pl.BlockSpec((tq, DN_D), lambda b, h, i: (b * nq + i, h)),
            pl.BlockSpec((None, tq, DR_D), lambda b, h, i: (h, b * nq + i, 0)),
            pl.BlockSpec((t, DN_D), lambda b, h, i: (b, h)),
            pl.BlockSpec((t, DR_D), lambda b, h, i: (b, 0)),
            pl.BlockSpec((t, DV_D), lambda b, h, i: (b, h)),
        ],
        out_specs=pl.BlockSpec((tq, DV_D), lambda b, h, i: (b * nq + i, h)),
        out_shape=jax.ShapeDtypeStruct((bsz * t, H_D * DV_D), F32),
        scratch_shapes=[
            pltpu.VMEM((tq, 1), F32),
            pltpu.VMEM((tq, 1), F32),
            pltpu.VMEM((tq, DV_D), F32),
        ],
        compiler_params=_compiler_params(("parallel", "parallel", "arbitrary")),
        name="mla_prompt",
    )(qn, qr, kn, kr, v)


SMP_G = 16
SMP_ROWS = 32
SMP_T = 4


def _page_specs(width, g_count, nj):
    def make(g):
        return pl.BlockSpec((None, PAGE_SIZE, width),
                            lambda b, j, pt: (pt[b, jnp.minimum(j, nj - 1) * g_count + g], 0, 0))

    return [make(g) for g in range(g_count)]


def _new_token_visible(shape):
    token = lax.broadcasted_iota(jnp.int32, shape, 0) & (SMP_T - 1)
    return lax.broadcasted_iota(jnp.int32, shape, 1) <= token


def _online_softmax_update(s, v, m_ref, l_ref, acc_ref):
    m_old = m_ref[...]
    m_new = jnp.maximum(m_old, jnp.max(s, axis=1, keepdims=True))
    p = jnp.exp(s - m_new)
    corr = jnp.exp(m_old - m_new)
    l_ref[...] = l_ref[...] * corr + jnp.sum(p, axis=1, keepdims=True)
    acc_ref[...] = acc_ref[...] * corr + jnp.dot(p.astype(BF16), v, preferred_element_type=F32)
    m_ref[...] = m_new


def _dsa_sample_select_body(ksel, g_count, pt_ref, qi_ref, wi_ref, knew_ref, *rest):
    k_refs = rest[:g_count]
    bias_ref, key_ref = rest[g_count:]
    j = pl.program_id(1)
    nj = pl.num_programs(1) - 1
    n = g_count * PAGE_SIZE
    contract_last = (((1,), (1,)), ((), ()))
    qi = qi_ref[...].astype(BF16)
    wi = wi_ref[...]

    def keys_of(kcat):
        o = lax.dot_general(qi, kcat, contract_last, preferred_element_type=F32)
        r = jnp.maximum(o, 0.0) * wi
        return _order_key(jnp.sum(r.reshape(8, HI_B, kcat.shape[0]), axis=1))

    @pl.when(j < nj)
    def _():
        key_ref[j] = keys_of(jnp.concatenate([r[...] for r in k_refs], 0).astype(BF16))

    @pl.when(j == nj)
    def _():
        new = jnp.where(_new_token_visible((8, PAGE_SIZE)), keys_of(knew_ref[...].astype(BF16)), INT32_MIN)
        key_ref[nj] = jnp.concatenate([new, jnp.full((8, n - PAGE_SIZE), INT32_MIN, jnp.int32)], 1)

        def count_ge(trial):
            def body(c, cnt):
                return cnt + (key_ref[c] >= trial).astype(jnp.int32)

            return jnp.sum(lax.fori_loop(0, nj + 1, body, jnp.zeros((8, n), jnp.int32)), axis=1, keepdims=True)

        def bit_pass(it, prefix):
            trial = prefix | lax.shift_left(jnp.int32(1), 31 - it)
            return jnp.where(count_ge(trial ^ INT32_MIN) >= ksel, trial, prefix)

        prefix = lax.fori_loop(0, 32, bit_pass, jnp.zeros((8, 1), jnp.int32))
        thr = jnp.maximum(prefix ^ INT32_MIN, INT32_MIN + 1)

        def write(c, carry):
            bias_ref[c] = jnp.where(key_ref[c] >= thr, 0.0, MASKED)
            return carry

        lax.fori_loop(0, nj + 1, write, 0)


def dsa_sample_select(page_table, qi8, wi8, ki_new, cache_kidx, g_count=SMP_G):
    bsz, n_pages = page_table.shape
    assert n_pages % g_count == 0
    nj = n_pages // g_count
    n = g_count * PAGE_SIZE
    ksel = min(TOPK_MAX, (n_pages * PAGE_SIZE + SMP_T) // 4)
    grid_spec = pltpu.PrefetchScalarGridSpec(
        num_scalar_prefetch=1,
        grid=(bsz, nj + 1),
        in_specs=[
            pl.BlockSpec((None, 8 * HI_B, DI_B), lambda b, j, pt: (b, 0, 0)),
            pl.BlockSpec((None, 8 * HI_B, 1), lambda b, j, pt: (b, 0, 0)),
            pl.BlockSpec((None, PAGE_SIZE, DI_B), lambda b, j, pt: (b, 0, 0)),
        ] + _page_specs(DI_B, g_count, nj),
        out_specs=pl.BlockSpec((None, nj + 1, 8, n), lambda b, j, pt: (b, 0, 0, 0)),
        scratch_shapes=[pltpu.VMEM((nj + 1, 8, n), jnp.int32)],
    )
    return pl.pallas_call(
        functools.partial(_dsa_sample_select_body, ksel, g_count),
        grid_spec=grid_spec,
        out_shape=jax.ShapeDtypeStruct((bsz, nj + 1, 8, n), F32),
        compiler_params=_compiler_params(("parallel", "arbitrary")),
        name="dsa_sample_select",
    )(page_table, qi8, wi8, ki_new, *([cache_kidx] * g_count))


def _dsa_sample_attn_body(g_count, pt_ref, q_ref, knew_ref, vnew_ref, bias_ref, *rest):
    k_refs = rest[:g_count]
    v_refs = rest[g_count:2 * g_count]
    o_ref, m_ref, l_ref, acc_ref = rest[2 * g_count:]
    j = pl.program_id(1)
    nj = pl.num_programs(1) - 1
    contract_last = (((1,), (1,)), ((), ()))
    q = q_ref[...].astype(BF16)

    @pl.when(j == 0)
    def _():
        m_ref[...] = jnp.full(m_ref.shape, MASKED, F32)
        l_ref[...] = jnp.zeros(l_ref.shape, F32)
        acc_ref[...] = jnp.zeros(acc_ref.shape, F32)

    def update(k, v, bias8):
        s = lax.dot_general(q, k, contract_last, preferred_element_type=F32) * ATT_SCALE_B
        s = s + jnp.concatenate([bias8] * (SMP_ROWS // 8), 0)
        _online_softmax_update(s, v, m_ref, l_ref, acc_ref)

    @pl.when(j < nj)
    def _():
        update(jnp.concatenate([r[...] for r in k_refs], 0).astype(BF16),
               jnp.concatenate([r[...] for r in v_refs], 0).astype(BF16), bias_ref[...])

    @pl.when(j == nj)
    def _():
        update(knew_ref[...].astype(BF16), vnew_ref[...].astype(BF16), bias_ref[:, :PAGE_SIZE])
        o_ref[...] = acc_ref[...] / l_ref[...]


def dsa_sample_attn(page_table, q_blk, k_new, v_new, bias, cache_k, cache_v, g_count=SMP_G):
    bsz, n_pages = page_table.shape
    nj = n_pages // g_count
    n = g_count * PAGE_SIZE
    kw = KVH_B * HD_B
    grid_spec = pltpu.PrefetchScalarGridSpec(
        num_scalar_prefetch=1,
        grid=(bsz, nj + 1),
        in_specs=[
            pl.BlockSpec((None, SMP_ROWS, kw), lambda b, j, pt: (b, 0, 0)),
            pl.BlockSpec((None, PAGE_SIZE, kw), lambda b, j, pt: (b, 0, 0)),
            pl.BlockSpec((None, PAGE_SIZE, kw), lambda b, j, pt: (b, 0, 0)),
            pl.BlockSpec((None, None, 8, n), lambda b, j, pt: (b, j, 0, 0)),
        ] + _page_specs(kw, g_count, nj) + _page_specs(kw, g_count, nj),
        out_specs=pl.BlockSpec((None, SMP_ROWS, kw), lambda b, j, pt: (b, 0, 0)),
        scratch_shapes=[pltpu.VMEM((SMP_ROWS, 1), F32), pltpu.VMEM((SMP_ROWS, 1), F32),
                        pltpu.VMEM((SMP_ROWS, kw), F32)],
    )
    return pl.pallas_call(
        functools.partial(_dsa_sample_attn_body, g_count),
        grid_spec=grid_spec,
        out_shape=jax.ShapeDtypeStruct((bsz, SMP_ROWS, kw), F32),
        compiler_params=_compiler_params(("parallel", "arbitrary")),
        name="dsa_sample_attn",
    )(page_table, q_blk, k_new, v_new, bias, *([cache_k] * g_count), *([cache_v] * g_count))


def _mla_sample_body(g_count, pt_ref, ql_ref, qr_ref, cnew_ref, rnew_ref, *rest):
    c_refs = rest[:g_count]
    r_refs = rest[g_count:2 * g_count]
    o_ref, m_ref, l_ref, acc_ref = rest[2 * g_count:]
    j = pl.program_id(1)
    nj = pl.num_programs(1) - 1
    contract_last = (((1,), (1,)), ((), ()))
    ql = ql_ref[...].astype(BF16)
    qr = qr_ref[...].astype(BF16)

    @pl.when(j == 0)
    def _():
        m_ref[...] = jnp.full(m_ref.shape, MASKED, F32)
        l_ref[...] = jnp.zeros(l_ref.shape, F32)
        acc_ref[...] = jnp.zeros(acc_ref.shape, F32)

    def update(c, r, new_tokens):
        s = (lax.dot_general(ql, c, contract_last, preferred_element_type=F32)
             + lax.dot_general(qr, r, contract_last, preferred_element_type=F32)) * ATT_SCALE_D
        if new_tokens:
            s = jnp.where(_new_token_visible(s.shape), s, MASKED)
        _online_softmax_update(s, c, m_ref, l_ref, acc_ref)

    @pl.when(j < nj)
    def _():
        update(jnp.concatenate([x[...] for x in c_refs], 0).astype(BF16),
               jnp.concatenate([x[...] for x in r_refs], 0).astype(BF16), False)

    @pl.when(j == nj)
    def _():
        update(cnew_ref[...].astype(BF16), rnew_ref[...].astype(BF16), True)
        o_ref[...] = acc_ref[...] / l_ref[...]


def mla_sample_attn(page_table, q_lat, q_rope, c_new, r_new, cache_ckv, cache_krope, g_count=SMP_G):
    bsz, n_pages = page_table.shape
    assert n_pages % g_count == 0
    nj = n_pages // g_count
    grid_spec = pltpu.PrefetchScalarGridSpec(
        num_scalar_prefetch=1,
        grid=(bsz, nj + 1),
        in_specs=[
            pl.BlockSpec((None, SMP_ROWS, R_KV), lambda b, j, pt: (b, 0, 0)),
            pl.BlockSpec((None, SMP_ROWS, DR_D), lambda b, j, pt: (b, 0, 0)),
            pl.BlockSpec((None, PAGE_SIZE, R_KV), lambda b, j, pt: (b, 0, 0)),
            pl.BlockSpec((None, PAGE_SIZE, DR_D), lambda b, j, pt: (b, 0, 0)),
        ] + _page_specs(R_KV, g_count, nj) + _page_specs(DR_D, g_count, nj),
        out_specs=pl.BlockSpec((None, SMP_ROWS, R_KV), lambda b, j, pt: (b, 0, 0)),
        scratch_shapes=[pltpu.VMEM((SMP_ROWS, 1), F32), pltpu.VMEM((SMP_ROWS, 1), F32),
                        pltpu.VMEM((SMP_ROWS, R_KV), F32)],
    )
    return pl.pallas_call(
        functools.partial(_mla_sample_body, g_count),
        grid_spec=grid_spec,
        out_shape=jax.ShapeDtypeStruct((bsz, SMP_ROWS, R_KV), F32),
        compiler_params=_compiler_params(("parallel", "arbitrary")),
        name="mla_sample_attn",
    )(page_table, q_lat, q_rope, c_new, r_new, *([cache_ckv] * g_count), *([cache_krope] * g_count))


PROJ_TM = 256


def _proj_ln_body(x_ref, o_in_ref, w_ref, g_ref, b_ref, o_ref):
    f = jnp.dot(o_in_ref[...].astype(BF16), w_ref[...], preferred_element_type=F32)
    y = DN_ALPHA * x_ref[...] + f
    o_ref[...] = _layer_norm_rows(y, g_ref[...], b_ref[...])


def proj_ln(x, o, w, g, b):
    m, d = x.shape
    k = o.shape[1]
    assert m % PROJ_TM == 0
    return pl.pallas_call(
        _proj_ln_body,
        grid=(m // PROJ_TM,),
        in_specs=[
            pl.BlockSpec((PROJ_TM, d), lambda i: (i, 0)),
            pl.BlockSpec((PROJ_TM, k), lambda i: (i, 0)),
            pl.BlockSpec((k, d), lambda i: (0, 0)),
            pl.BlockSpec((1, d), lambda i: (0, 0)),
            pl.BlockSpec((1, d), lambda i: (0, 0)),
        ],
        out_specs=pl.BlockSpec((PROJ_TM, d), lambda i: (i, 0)),
        out_shape=jax.ShapeDtypeStruct((m, d), F32),
        compiler_params=_compiler_params(("parallel",)),
        name="proj_ln",
    )(x, o, w, g, b)


def _l2_norm(x):
    return x * lax.rsqrt(jnp.sum(x * x, -1, keepdims=True) + RMS_EPS)


def _rms_norm(x, g):
    return x * lax.rsqrt(jnp.mean(x * x, -1, keepdims=True) + RMS_EPS) * g


def _causal_conv(x, buf, w):
    t = x.shape[1]
    xp = jnp.concatenate([buf.astype(x.dtype), x], axis=1)
    y = sum(xp[:, j:j + t] * w[j] for j in range(CONV_W))
    return y, xp[:, t:]


def _rope(x, pos):
    half = DR_D // 2
    inv = ROPE_THETA ** (-jnp.arange(half, dtype=F32) / half)
    ang = pos.astype(F32)[:, None] * inv
    ang = ang.reshape(ang.shape[0], *([1] * (x.ndim - 3)), half)
    cos, sin = jnp.cos(ang), jnp.sin(ang)
    x1, x2 = x[..., :half], x[..., half:]
    return jnp.concatenate([x1 * cos - x2 * sin, x2 * cos + x1 * sin], -1)


def _chunk(a, n, c):
    a = jnp.pad(a, [(0, 0), (0, n * c - a.shape[1])] + [(0, 0)] * (a.ndim - 2))
    a = a.reshape(a.shape[0], n, c, *a.shape[2:])
    return jnp.swapaxes(jnp.swapaxes(a, 0, 1), 2, 3)


def _gated_delta(q, k, v, g, beta, s0):
    bsz, t = q.shape[:2]
    c = min(GDN_CHUNK, t)
    n = -(-t // c)
    qc = _chunk(q * DK_A ** -0.5, n, c)
    kc, vc = _chunk(k, n, c), _chunk(v, n, c)
    gc, bc = _chunk(g, n, c), _chunk(beta, n, c)
    gcum = jnp.cumsum(gc, axis=-1)
    incl = jnp.tril(jnp.ones((c, c), bool))
    strict = jnp.tril(jnp.ones((c, c), bool), -1)
    decay = jnp.exp(jnp.where(incl, gcum[..., :, None] - gcum[..., None, :], -jnp.inf))
    kb = kc * bc[..., None]
    lmat = jnp.where(strict, jnp.einsum('nbhid,nbhjd->nbhij', kb, kc) * decay, 0.0)
    eye = jnp.eye(c, dtype=F32)
    tmat = lax.linalg.triangular_solve(eye + lmat, jnp.broadcast_to(eye, lmat.shape),
                                       left_side=True, lower=True, unit_diagonal=True)
    u = tmat @ (vc * bc[..., None])
    w = tmat @ (kb * jnp.exp(gcum)[..., None])
    qk = jnp.einsum('nbhid,nbhjd->nbhij', qc, kc) * decay
    qg = qc * jnp.exp(gcum)[..., None]
    kd = kc * jnp.exp(gcum[..., -1:] - gcum)[..., None]
    glast = jnp.exp(gcum[..., -1])

    def step(s, xs):
        u_i, w_i, qk_i, qg_i, kd_i, gl_i = xs
        v_new = u_i - jnp.einsum('bhcd,bhde->bhce', w_i, s)
        o = jnp.einsum('bhcd,bhde->bhce', qg_i, s) + jnp.einsum('bhij,bhje->bhie', qk_i, v_new)
        s = s * gl_i[..., None, None] + jnp.einsum('bhcd,bhce->bhde', kd_i, v_new)
        return s, o

    s, o = lax.scan(step, s0, (u, w, qk, qg, kd, glast))
    o = jnp.swapaxes(jnp.swapaxes(o, 2, 3), 0, 1).reshape(bsz, n * c, H_A, DV_A)[:, :t]
    return o, s


def _pad_page(a):
    return jnp.pad(a, ((0, 0), (0, PAGE_SIZE - a.shape[1]), (0, 0)))


def _dsa_sample(q, k, v, qi, ki, wi, cache_k, cache_v, cache_kidx, page_table, g_count=SMP_G):
    bsz, t = q.shape[:2]
    n_grp = H_B // KVH_B
    kw = KVH_B * HD_B
    qi8 = jnp.concatenate([qi, qi], 1).reshape(bsz, 2 * t * HI_B, DI_B)
    wi8 = jnp.concatenate([wi, wi], 1).reshape(bsz, 2 * t * HI_B, 1)
    bias = dsa_sample_select(page_table, qi8, wi8, _pad_page(ki), cache_kidx, g_count)
    qg = jnp.transpose(q.reshape(bsz, t, KVH_B, n_grp, HD_B), (0, 2, 3, 1, 4)).reshape(bsz, KVH_B, n_grp * t, HD_B)
    own_lanes = jnp.eye(KVH_B, dtype=bool)[None, :, None, :, None]
    q_blk = jnp.where(own_lanes, qg[:, :, :, None, :], 0.0).reshape(bsz, KVH_B * n_grp * t, kw)
    out = dsa_sample_attn(page_table, q_blk, _pad_page(k.reshape(bsz, t, kw)), _pad_page(v.reshape(bsz, t, kw)), bias,
                          cache_k.reshape(-1, PAGE_SIZE, kw), cache_v.reshape(-1, PAGE_SIZE, kw), g_count)
    out = out.reshape(bsz, KVH_B, n_grp, t, KVH_B, HD_B)
    out = jnp.stack([out[:, g, :, :, g, :] for g in range(KVH_B)], 1)
    return jnp.transpose(out, (0, 3, 1, 2, 4)).reshape(bsz, t, H_B, HD_B)


def _mixer_ab_core(hm, hs, conv_buf, rec_state, sparse_attn, conv_w_a, a_log_a, dt_bias_a, norm_g_a):
    bsz, t, _ = hm.shape
    qkv_pre = hm[..., :A_QKV]
    z = hm[..., A_QKV:A_QKV + 1024]
    q_b = hm[..., 4096:5120]
    qi_b = hm[..., 5120:6144]
    k_b = hm[..., 6144:6400]
    v_b = hm[..., 6400:6656]
    ki_b = hs[..., :DI_B]
    b_gate = hs[..., 64:72]
    a_gate = hs[..., 72:80]
    wi_b = hs[..., 80:96]
    qkv, conv_new = _causal_conv(qkv_pre, conv_buf, conv_w_a)
    qkv = jax.nn.silu(qkv)
    qa = _l2_norm(qkv[..., :1024].reshape(bsz, t, H_A, DK_A))
    ka = _l2_norm(qkv[..., 1024:2048].reshape(bsz, t, H_A, DK_A))
    va = qkv[..., 2048:].reshape(bsz, t, H_A, DV_A)
    beta = jax.nn.sigmoid(b_gate)
    g = -jnp.exp(a_log_a) * jax.nn.softplus(a_gate + dt_bias_a)
    o_a, s_new = _gated_delta(qa, ka, va, g, beta, rec_state)
    o_a = _rms_norm(o_a, norm_g_a) * jax.nn.silu(z.reshape(bsz, t, H_A, DV_A))
    q_b = q_b.reshape(bsz, t, H_B, HD_B)
    k_b = k_b.reshape(bsz, t, KVH_B, HD_B)
    v_b = v_b.reshape(bsz, t, KVH_B, HD_B)
    qi_b = qi_b.reshape(bsz, t, HI_B, DI_B)
    o_b = sparse_attn(q_b, k_b, v_b, qi_b, ki_b, wi_b * IDX_SCALE)
    o = jnp.concatenate([o_a.reshape(bsz, t, -1), o_b.reshape(bsz, t, -1)], -1)
    return o, conv_new, s_new, k_b, v_b, ki_b


def _rglru_branch(xb, yb, conv_buf, h0, conv_w_c, conv_b_c, w_rg_a, b_rg_a, w_rg_x, b_rg_x, lambda_c):
    bsz, t, _ = xb.shape
    xc, conv_new = _causal_conv(xb, conv_buf, conv_w_c)
    xc = xc + conv_b_c
    xr = xc.reshape(bsz, t, NB_C, BW_C)
    r = jax.nn.sigmoid(jnp.einsum('btnd,nde->btne', xr, w_rg_a).reshape(bsz, t, D_C) + b_rg_a)
    i = jax.nn.sigmoid(jnp.einsum('btnd,nde->btne', xr, w_rg_x).reshape(bsz, t, D_C) + b_rg_x)
    log_a = -RG_C * r * jax.nn.softplus(-lambda_c)
    a = jnp.exp(log_a)
    bterm = jnp.sqrt(-jnp.expm1(2.0 * log_a)) * (i * xc)
    bterm = bterm.at[:, 0].add(a[:, 0] * h0)

    def comb(lhs, rhs):
        return lhs[0] * rhs[0], rhs[0] * lhs[1] + rhs[1]

    _, h = lax.associative_scan(comb, (a, bterm), axis=1)
    return h * jax.nn.gelu(yb), conv_new, h[:, -1]


def _mla_prompt(qn, qr, ckv, kr, w_uk, w_uv):
    bsz, s_len = qn.shape[:2]
    m = bsz * s_len
    ckv2 = ckv.reshape(m, R_KV)
    k_nope = matmul(ckv2, w_uk.reshape(R_KV, H_D * DN_D).astype(BF16), H_D * DN_D, BF16)
    v = matmul(ckv2, w_uv.reshape(R_KV, H_D * DV_D).astype(BF16), H_D * DV_D, BF16)
    qr_hm = jnp.moveaxis(qr.reshape(m, H_D, DR_D), 1, 0)
    out = mla_prompt(qn.reshape(m, H_D * DN_D), qr_hm, k_nope, kr.reshape(m, DR_D).astype(BF16), v, bsz, s_len)
    return out.reshape(bsz, s_len, H_D, DV_D)


def _mla_sample(qn, qr, ckv, kr, w_uk, w_uv, cache_ckv, cache_krope, page_table, g_count=SMP_G):
    bsz, t = qn.shape[:2]
    q_lat = jnp.einsum('bthd,rhd->bthr', qn, w_uk)
    rows = lambda a: jnp.swapaxes(a, 1, 2).reshape(bsz, H_D * t, a.shape[-1])
    lat = mla_sample_attn(page_table, rows(q_lat), rows(qr), _pad_page(ckv), _pad_page(kr),
                          cache_ckv, cache_krope, g_count)
    lat = jnp.swapaxes(lat.reshape(bsz, H_D, t, R_KV), 1, 2)
    return jnp.einsum('bthr,rhd->bthd', lat, w_uv)


def _mixer_cd_core(hm, hs, conv_buf, h0, pos0, latent_attn, conv_w_c, conv_b_c, w_rg_a, b_rg_a,
                   w_rg_x, b_rg_x, lambda_c, kv_norm_g_d):
    bsz, t, _ = hm.shape
    xb = hm[..., :D_C]
    yb = hm[..., D_C:2 * D_C]
    qn = hm[..., 2048:3072].reshape(bsz, t, H_D, DN_D)
    qr = hm[..., 3072:3584].reshape(bsz, t, H_D, DR_D)
    ckv = hm[..., 3584:3840]
    kr = hs[..., :DR_D]
    o_c, conv_new, h_new = _rglru_branch(xb, yb, conv_buf, h0, conv_w_c, conv_b_c, w_rg_a, b_rg_a,
                                         w_rg_x, b_rg_x, lambda_c)
    pos = pos0 + jnp.arange(t)
    qr = _rope(qr, pos)
    kr = _rope(kr, pos)
    ckv = _rms_norm(ckv, kv_norm_g_d)
    o_d = latent_attn(qn, qr, ckv, kr)
    o = jnp.concatenate([o_c, o_d.reshape(bsz, t, -1)], -1)
    return o, conv_new, h_new, ckv, kr


def _pad_cols(w, n):
    return jnp.pad(w, ((0, 0), (0, n - w.shape[1])))


def kernel(x_prompt, x_sample, state_a_conv, state_a_rec, cache_b_k, cache_b_v, cache_b_kidx, state_c_conv, state_c_h, cache_d_ckv, cache_d_krope, page_table, ln_g, ln_b, ffn_w_gate, ffn_w_up, ffn_w_down, w_in_ab, conv_w_a, a_log_a, dt_bias_a, norm_g_a, w_out_ab, w_in_cd, conv_w_c, conv_b_c, w_rg_a, b_rg_a, w_rg_x, b_rg_x, lambda_c, kv_norm_g_d, w_uk_d, w_uv_d, w_out_cd):
    bp, tp = x_prompt.shape[:2]
    bs, ts = x_sample.shape[:2]
    mp = bp * tp
    past = page_table.shape[1] * PAGE_SIZE
    n_pg = tp // PAGE_SIZE

    x = jnp.concatenate([x_prompt.reshape(mp, D_MODEL), x_sample.reshape(bs * ts, D_MODEL)], 0)

    wg = ffn_w_gate.astype(BF16)
    wu = ffn_w_up.astype(BF16)
    wd = ffn_w_down.astype(BF16)

    ab = np.cumsum((0, A_QKV, H_A * DV_A, H_A, H_A, H_B * HD_B, KVH_B * HD_B, KVH_B * HD_B, HI_B * DI_B, DI_B, HI_B))
    col = lambda w, i: w[:, int(ab[i]):int(ab[i + 1])]
    w_ab_main = jnp.concatenate([col(w_in_ab, 0), col(w_in_ab, 1), col(w_in_ab, 4), col(w_in_ab, 7),
                                 col(w_in_ab, 5), col(w_in_ab, 6)], 1).astype(BF16)
    w_ab_small = _pad_cols(jnp.concatenate([col(w_in_ab, 8), col(w_in_ab, 2), col(w_in_ab, 3), col(w_in_ab, 9)], 1),
                           LANE).astype(BF16)
    w_q_d = w_in_cd[:, 2 * D_C:2 * D_C + H_D * (DN_D + DR_D)].reshape(D_MODEL, H_D, DN_D + DR_D)
    w_cd_main = jnp.concatenate([w_in_cd[:, :2 * D_C],
                                 w_q_d[..., :DN_D].reshape(D_MODEL, H_D * DN_D),
                                 w_q_d[..., DN_D:].reshape(D_MODEL, H_D * DR_D),
                                 w_in_cd[:, 3584:3840]], 1).astype(BF16)
    w_cd_small = _pad_cols(w_in_cd[:, 3840:], LANE).astype(BF16)
    w_out_ab_b = w_out_ab.astype(BF16)
    w_out_cd_b = w_out_cd.astype(BF16)

    def ffn(x, layer, half, ln_idx):
        return ffn_ln(x, wg[layer, half], wu[layer, half], wd[layer, half],
                      ln_g[layer, ln_idx][None], ln_b[layer, ln_idx][None])

    x = ffn(x, 0, 0, 0)
    hm = matmul(x, w_ab_main, 512)
    hs = matmul(x, w_ab_small, LANE)
    o_p, a_conv_p, a_rec_p, kb_p, vb_p, kib_p = _mixer_ab_core(
        hm[:mp].reshape(bp, tp, -1), hs[:mp].reshape(bp, tp, -1),
        jnp.zeros((bp, CONV_W - 1, A_QKV), F32), jnp.zeros((bp, H_A, DK_A, DV_A), F32),
        lambda *unused: dsa_prompt(hm, hs, bp, tp), conv_w_a, a_log_a, dt_bias_a, norm_g_a)
    attn_s = functools.partial(_dsa_sample, cache_k=cache_b_k, cache_v=cache_b_v,
                               cache_kidx=cache_b_kidx, page_table=page_table)
    o_s, a_conv_s, a_rec_s, b_k_s, b_v_s, b_kidx_s = _mixer_ab_core(
        hm[mp:].reshape(bs, ts, -1), hs[mp:].reshape(bs, ts, -1), state_a_conv, state_a_rec,
        attn_s, conv_w_a, a_log_a, dt_bias_a, norm_g_a)
    b_k_p = kb_p.reshape(bp, n_pg, PAGE_SIZE, KVH_B, HD_B)
    b_v_p = vb_p.reshape(bp, n_pg, PAGE_SIZE, KVH_B, HD_B)
    b_kidx_p = kib_p.reshape(bp, n_pg, PAGE_SIZE, DI_B)
    o = jnp.concatenate([o_p.reshape(mp, -1), o_s.reshape(bs * ts, -1)], 0)
    x = proj_ln(x, o, w_out_ab_b, ln_g[0, 1][None], ln_b[0, 1][None])
    x = ffn(x, 0, 1, 2)

    x = ffn(x, 1, 0, 0)
    hm = matmul(x, w_cd_main, 768)
    hs = matmul(x, w_cd_small, LANE)
    attn_p = functools.partial(_mla_prompt, w_uk=w_uk_d, w_uv=w_uv_d)
    o_p, c_conv_p, c_h_p, ckv_p, kr_p = _mixer_cd_core(
        hm[:mp].reshape(bp, tp, -1), hs[:mp].reshape(bp, tp, -1),
        jnp.zeros((bp, CONV_W - 1, D_C), F32), jnp.zeros((bp, D_C), F32), 0, attn_p,
        conv_w_c, conv_b_c, w_rg_a, b_rg_a, w_rg_x, b_rg_x, lambda_c, kv_norm_g_d)
    attn_s = functools.partial(_mla_sample, w_uk=w_uk_d, w_uv=w_uv_d, cache_ckv=cache_d_ckv,
                               cache_krope=cache_d_krope, page_table=page_table)
    o_s, c_conv_s, c_h_s, d_ckv_s, d_krope_s = _mixer_cd_core(
        hm[mp:].reshape(bs, ts, -1), hs[mp:].reshape(bs, ts, -1), state_c_conv, state_c_h, past, attn_s,
        conv_w_c, conv_b_c, w_rg_a, b_rg_a, w_rg_x, b_rg_x, lambda_c, kv_norm_g_d)
    d_ckv_p = ckv_p.reshape(bp, n_pg, PAGE_SIZE, R_KV)
    d_krope_p = kr_p.reshape(bp, n_pg, PAGE_SIZE, DR_D)
    o = jnp.concatenate([o_p.reshape(mp, -1), o_s.reshape(bs * ts, -1)], 0)
    x = proj_ln(x, o, w_out_cd_b, ln_g[1, 1][None], ln_b[1, 1][None])
    x = ffn(x, 1, 1, 2)

    yp = x[:mp].reshape(bp, tp, D_MODEL)
    ys = x[mp:].reshape(bs, ts, D_MODEL)
    return (yp, ys,
            a_conv_p, a_conv_s, a_rec_p, a_rec_s,
            b_k_p, b_k_s, b_v_p, b_v_s, b_kidx_p, b_kidx_s,
            c_conv_p, c_conv_s, c_h_p, c_h_s,
            d_ckv_p, d_ckv_s, d_krope_p, d_krope_s)
```

```python
import functools

import jax
import jax.numpy as jnp
import numpy as np
from jax import lax
from jax.experimental import pallas as pl
from jax.experimental.pallas import tpu as pltpu

D_MODEL = 2048
DEPTH = 2
PAGE_SIZE = 128
H_A = 8
DK_A = 128
DV_A = 128
CONV_W = 4
GDN_CHUNK = 64
A_QKV = 2 * H_A * DK_A + H_A * DV_A
H_B = 8
KVH_B = 2
HD_B = 128
HI_B = 16
DI_B = 64
TOPK_MAX = 256
D_C = 1024
NB_C = 8
BW_C = D_C // NB_C
RG_C = 8.0
H_D = 8
DN_D = 128
DR_D = 64
DV_D = 128
R_KV = 256
ROPE_THETA = 10000.0
D_FF = 5632
Q_BLOCK = 128
LN_EPS = 1e-5
RMS_EPS = 1e-6
DN_ALPHA = (2 * DEPTH) ** 0.25
IDX_SCALE = (HI_B * DI_B) ** -0.5
ATT_SCALE_B = HD_B ** -0.5
ATT_SCALE_D = (DN_D + DR_D) ** -0.5
F32 = jnp.float32
BF16 = jnp.bfloat16

VMEM_LIMIT_BYTES = 56 * 1024 * 1024
LANE = 128


def _compiler_params(semantics):
    return pltpu.CompilerParams(dimension_semantics=semantics, vmem_limit_bytes=VMEM_LIMIT_BYTES)


def _layer_norm_rows(y, g, b):
    mu = jnp.mean(y, axis=-1, keepdims=True)
    d = y - mu
    var = jnp.mean(d * d, axis=-1, keepdims=True)
    return d * lax.rsqrt(var + LN_EPS) * g + b


FFN_TM = 512
FFN_TF = 512


def _ffn_ln_body(x_ref, wg_ref, wu_ref, wd_ref, g_ref, b_ref, o_ref, xb_ref, acc_ref):
    j = pl.program_id(1)

    @pl.when(j == 0)
    def _():
        xb_ref[...] = x_ref[...].astype(BF16)
        acc_ref[...] = jnp.zeros_like(acc_ref)

    xb = xb_ref[...]
    hg = jnp.dot(xb, wg_ref[...], preferred_element_type=F32)
    hu = jnp.dot(xb, wu_ref[...], preferred_element_type=F32)
    h = (hg * jax.nn.sigmoid(hg)) * hu
    acc_ref[...] += jnp.dot(h.astype(BF16), wd_ref[...], preferred_element_type=F32)

    @pl.when(j == pl.num_programs(1) - 1)
    def _():
        y = DN_ALPHA * x_ref[...] + 0.5 * acc_ref[...]
        o_ref[...] = _layer_norm_rows(y, g_ref[...], b_ref[...])


def ffn_ln(x, wg, wu, wd, g, b):
    m, d = x.shape
    f = wg.shape[1]
    assert m % FFN_TM == 0 and f % FFN_TF == 0
    return pl.pallas_call(
        _ffn_ln_body,
        grid=(m // FFN_TM, f // FFN_TF),
        in_specs=[
            pl.BlockSpec((FFN_TM, d), lambda i, j: (i, 0)),
            pl.BlockSpec((d, FFN_TF), lambda i, j: (0, j)),
            pl.BlockSpec((d, FFN_TF), lambda i, j: (0, j)),
            pl.BlockSpec((FFN_TF, d), lambda i, j: (j, 0)),
            pl.BlockSpec((1, d), lambda i, j: (0, 0)),
            pl.BlockSpec((1, d), lambda i, j: (0, 0)),
        ],
        out_specs=pl.BlockSpec((FFN_TM, d), lambda i, j: (i, 0)),
        out_shape=jax.ShapeDtypeStruct((m, d), F32),
        scratch_shapes=[pltpu.VMEM((FFN_TM, d), BF16), pltpu.VMEM((FFN_TM, d), F32)],
        compiler_params=_compiler_params(("parallel", "arbitrary")),
        name="ffn_ln",
    )(x, wg, wu, wd, g, b)


MM_TM = 512


def _matmul_body(x_ref, w_ref, o_ref, xb_ref):
    @pl.when(pl.program_id(1) == 0)
    def _():
        xb_ref[...] = x_ref[...].astype(BF16)

    o_ref[...] = jnp.dot(xb_ref[...], w_ref[...], preferred_element_type=F32).astype(o_ref.dtype)


def matmul(x, w, tn, out_dtype=F32):
    m, k = x.shape
    n = w.shape[1]
    assert m % MM_TM == 0 and n % tn == 0
    return pl.pallas_call(
        _matmul_body,
        grid=(m // MM_TM, n // tn),
        in_specs=[
            pl.BlockSpec((MM_TM, k), lambda i, j: (i, 0)),
            pl.BlockSpec((k, tn), lambda i, j: (0, j)),
        ],
        out_specs=pl.BlockSpec((MM_TM, tn), lambda i, j: (i, j)),
        out_shape=jax.ShapeDtypeStruct((m, n), out_dtype),
        scratch_shapes=[pltpu.VMEM((MM_TM, k), BF16)],
        compiler_params=_compiler_params(("parallel", "arbitrary")),
        name="matmul",
    )(x, w)


DSA_TQ = 128
DSA_KC = 512
MASKED = -1e30
INT32_MIN = -2 ** 31
AB_Q_COL, AB_QI_COL, AB_K_COL, AB_V_COL = 4096, 5120, 6144, 6400
AB_WI_LANE = 80


def _order_key(x):
    bits = lax.bitcast_convert_type(jnp.where(x == 0.0, 0.0, x), jnp.int32)
    return jnp.where(bits < 0, bits ^ 0x7FFFFFFF, bits)


def _dsa_prompt_body(ksel, q_ref, qi_ref, hsq_ref, k_ref, v_ref, hsk_ref, o_ref,
                     kb_ref, vb_ref, kk_ref, keyt_ref, bias_ref, m_ref, l_ref, acc_ref):
    i = pl.program_id(1)
    tq, kc = DSA_TQ, DSA_KC
    n_all = kk_ref.shape[0] // kc
    n_grp = H_B // KVH_B
    contract_last = (((1,), (1,)), ((), ()))

    @pl.when(i == 0)
    def _():
        def cast_chunk(c, carry):
            rows = pl.ds(pl.multiple_of(c * kc, kc), kc)
            kf = k_ref[rows, :]
            vf = v_ref[rows, :]
            for g in range(KVH_B):
                kb_ref[g, rows, :] = kf[:, g * HD_B:(g + 1) * HD_B].astype(BF16)
                vb_ref[g, rows, :] = vf[:, g * HD_B:(g + 1) * HD_B].astype(BF16)
            hs = hsk_ref[rows, :]
            lane = lax.broadcasted_iota(jnp.int32, hs.shape, 1)
            kk_ref[rows, :] = jnp.where(lane < DI_B, hs, pltpu.roll(hs, DI_B, 1)).astype(BF16)
            return carry

        lax.fori_loop(0, n_all, cast_chunk, 0)

    nk = (i * tq + tq + kc - 1) // kc

    qi = qi_ref[...]
    lane = lax.broadcasted_iota(jnp.int32, (tq, 2 * DI_B), 1)
    rhs = []
    for j in range(HI_B // 2):
        pair = qi[:, j * 2 * DI_B:(j + 1) * 2 * DI_B]
        rhs.append(jnp.concatenate([jnp.where(lane < DI_B, pair, 0.0),
                                    jnp.where(lane >= DI_B, pair, 0.0)], 0).astype(BF16))
    wi_t = jnp.transpose(hsq_ref[...])[AB_WI_LANE:AB_WI_LANE + HI_B, :] * IDX_SCALE
    t_pos = i * tq + lax.broadcasted_iota(jnp.int32, (kc, tq), 1)

    def score_chunk(c, carry):
        kk = kk_ref[pl.ds(pl.multiple_of(c * kc, kc), kc), :]
        sc = jnp.zeros((kc, tq), F32)
        for j in range(HI_B // 2):
            o = lax.dot_general(kk, rhs[j], contract_last, preferred_element_type=F32)
            sc = sc + wi_t[2 * j:2 * j + 1, :] * jnp.maximum(o[:, :tq], 0.0)
            sc = sc + wi_t[2 * j + 1:2 * j + 2, :] * jnp.maximum(o[:, tq:], 0.0)
        s_pos = c * kc + lax.broadcasted_iota(jnp.int32, (kc, tq), 0)
        keyt_ref[c] = jnp.where(s_pos <= t_pos, _order_key(sc), INT32_MIN)
        return carry

    lax.fori_loop(0, nk, score_chunk, 0)

    def count_ge(trial):
        def body(c, cnt):
            ge = (keyt_ref[c] >= trial).astype(jnp.int32)
            return cnt + jnp.sum(ge.reshape(kc // 8, 8, tq), axis=0)

        cnt = lax.fori_loop(0, nk, body, jnp.zeros((8, tq), jnp.int32))
        return jnp.sum(cnt, axis=0, keepdims=True)

    def bit_pass(it, prefix):
        trial = prefix | lax.shift_left(jnp.int32(1), 31 - it)
        return jnp.where(count_ge(trial ^ INT32_MIN) >= ksel, trial, prefix)

    prefix = lax.fori_loop(0, 32, bit_pass, jnp.zeros((1, tq), jnp.int32))
    thr = jnp.maximum(prefix ^ INT32_MIN, INT32_MIN + 1)

    def bias_chunk(c, carry):
        bias_ref[c] = jnp.transpose(jnp.where(keyt_ref[c] >= thr, 0.0, MASKED))
        return carry

    lax.fori_loop(0, nk, bias_chunk, 0)

    q = q_ref[...]
    qg = [jnp.concatenate([q[:, (g * n_grp + hh) * HD_B:(g * n_grp + hh + 1) * HD_B] for hh in range(n_grp)],
                          0).astype(BF16) for g in range(KVH_B)]
    m_ref[...] = jnp.full(m_ref.shape, MASKED, F32)
    l_ref[...] = jnp.zeros(l_ref.shape, F32)
    acc_ref[...] = jnp.zeros(acc_ref.shape, F32)

    def attn_chunk(c, carry):
        rows = pl.ds(pl.multiple_of(c * kc, kc), kc)
        bias = bias_ref[c]
        for g in range(KVH_B):
            s = lax.dot_general(qg[g], kb_ref[g, rows, :], contract_last, preferred_element_type=F32) * ATT_SCALE_B
            s = (s.reshape(n_grp, tq, kc) + bias[None]).reshape(n_grp * tq, kc)
            m_old = m_ref[g]
            m_new = jnp.maximum(m_old, jnp.max(s, axis=1, keepdims=True))
            p = jnp.exp(s - m_new)
            corr = jnp.exp(m_old - m_new)
            l_ref[g] = l_ref[g] * corr + jnp.sum(p, axis=1, keepdims=True)
            acc_ref[g] = acc_ref[g] * corr + jnp.dot(p.astype(BF16), vb_ref[g, rows, :], preferred_element_type=F32)
            m_ref[g] = m_new
        return carry

    lax.fori_loop(0, nk, attn_chunk, 0)

    for g in range(KVH_B):
        o = acc_ref[g] / l_ref[g]
        for hh in range(n_grp):
            o_ref[:, (g * n_grp + hh) * HD_B:(g * n_grp + hh + 1) * HD_B] = o[hh * tq:(hh + 1) * tq, :]


def dsa_prompt(hm, hs, bsz, t):
    tq, kc = DSA_TQ, DSA_KC
    assert t % kc == 0 and kc % tq == 0
    nq = t // tq
    ksel = min(TOPK_MAX, t // 4)
    qw = H_B * HD_B
    kw = KVH_B * HD_B
    n_grp = H_B // KVH_B
    return pl.pallas_call(
        functools.partial(_dsa_prompt_body, ksel),
        grid=(bsz, nq),
        in_specs=[
            pl.BlockSpec((tq, qw), lambda b, i: (b * nq + i, AB_Q_COL // qw)),
            pl.BlockSpec((tq, HI_B * DI_B), lambda b, i: (b * nq + i, AB_QI_COL // (HI_B * DI_B))),
            pl.BlockSpec((tq, LANE), lambda b, i: (b * nq + i, 0)),
            pl.BlockSpec((t, kw), lambda b, i: (b, AB_K_COL // kw)),
            pl.BlockSpec((t, kw), lambda b, i: (b, AB_V_COL // kw)),
            pl.BlockSpec((t, LANE), lambda b, i: (b, 0)),
        ],
        out_specs=pl.BlockSpec((tq, qw), lambda b, i: (b * nq + i, 0)),
        out_shape=jax.ShapeDtypeStruct((bsz * t, qw), F32),
        scratch_shapes=[
            pltpu.VMEM((KVH_B, t, HD_B), BF16),
            pltpu.VMEM((KVH_B, t, HD_B), BF16),
            pltpu.VMEM((t, LANE), BF16),
            pltpu.VMEM((t // kc, kc, tq), jnp.int32),
            pltpu.VMEM((t // kc, tq, kc), F32),
            pltpu.VMEM((KVH_B, n_grp * tq, 1), F32),
            pltpu.VMEM((KVH_B, n_grp * tq, 1), F32),
            pltpu.VMEM((KVH_B, n_grp * tq, HD_B), F32),
        ],
        compiler_params=_compiler_params(("parallel", "arbitrary")),
        name="dsa_prompt",
    )(hm, hm, hs, hm, hm, hs)


MLA_TQ = 512


def _mla_prompt_body(qn_ref, qr_ref, kn_ref, kr_ref, v_ref, o_ref, m_ref, l_ref, acc_ref):
    i = pl.program_id(2)
    tq = MLA_TQ
    contract_last = (((1,), (1,)), ((), ()))
    qn = qn_ref[...].astype(BF16)
    qr = qr_ref[...].astype(BF16)
    m_ref[...] = jnp.full(m_ref.shape, MASKED, F32)
    l_ref[...] = jnp.zeros(l_ref.shape, F32)
    acc_ref[...] = jnp.zeros(acc_ref.shape, F32)

    def chunk(c, diagonal):
        rows = pl.ds(pl.multiple_of(c * tq, tq), tq)
        s = (lax.dot_general(qn, kn_ref[rows, :], contract_last, preferred_element_type=F32)
             + lax.dot_general(qr, kr_ref[rows, :], contract_last, preferred_element_type=F32)) * ATT_SCALE_D
        if diagonal:
            causal = (lax.broadcasted_iota(jnp.int32, (tq, tq), 1) <= lax.broadcasted_iota(jnp.int32, (tq, tq), 0))
            s = jnp.where(causal, s, MASKED)
        m_old = m_ref[...]
        m_new = jnp.maximum(m_old, jnp.max(s, axis=1, keepdims=True))
        p = jnp.exp(s - m_new)
        corr = jnp.exp(m_old - m_new)
        l_ref[...] = l_ref[...] * corr + jnp.sum(p, axis=1, keepdims=True)
        acc_ref[...] = acc_ref[...] * corr + jnp.dot(p.astype(BF16), v_ref[rows, :], preferred_element_type=F32)
        m_ref[...] = m_new

    def below_diagonal(c, carry):
        chunk(c, False)
        return carry

    lax.fori_loop(0, i, below_diagonal, 0)
    chunk(i, True)
    o_ref[...] = acc_ref[...] / l_ref[...]


def mla_prompt(qn, qr, kn, kr, v, bsz, t):
    tq = MLA_TQ
    assert t % tq == 0
    nq = t // tq
    return pl.pallas_call(
        _mla_prompt_body,
        grid=(bsz, H_D, nq),
        in_specs=[
            pl.BlockSpec((tq, DN_D), lambda b, h, i: (b * nq + i, h)),
            pl.BlockSpec((None, tq, DR_D), lambda b, h, i: (h, b * nq + i, 0)),
            pl.BlockSpec((t, DN_D), lambda b, h, i: (b, h)),
            pl.BlockSpec((t, DR_D), lambda b, h, i: (b, 0)),
            pl.BlockSpec((t, DV_D), lambda b, h, i: (b, h)),
        ],
        out_specs=pl.BlockSpec((tq, DV_D), lambda b, h, i: (b * nq + i, h)),
        out_shape=jax.ShapeDtypeStruct((bsz * t, H_D * DV_D), F32),
        scratch_shapes=[
            pltpu.VMEM((tq, 1), F32),
            pltpu.VMEM((tq, 1), F32),
            pltpu.VMEM((tq, DV_D), F32),
        ],
        compiler_params=_compiler_params(("parallel", "parallel", "arbitrary")),
        name="mla_prompt",
    )(qn, qr, kn, kr, v)


SMP_G = 16
SMP_ROWS = 32
SMP_T = 4


def _page_specs(page_shape, g_count, nj):
    zeros = (0,) * len(page_shape)

    def make(g):
        return pl.BlockSpec((None,) + page_shape,
                            lambda b, j, pt: (pt[b, jnp.minimum(j, nj - 1) * g_count + g],) + zeros)

    return [make(g) for g in range(g_count)]


def _new_token_visible(shape):
    token = lax.broadcasted_iota(jnp.int32, shape, 0) & (SMP_T - 1)
    return lax.broadcasted_iota(jnp.int32, shape, 1) <= token


def _online_softmax_update(s, v, m_ref, l_ref, acc_ref):
    m_old = m_ref[...]
    m_new = jnp.maximum(m_old, jnp.max(s, axis=1, keepdims=True))
    p = jnp.exp(s - m_new)
    corr = jnp.exp(m_old - m_new)
    l_ref[...] = l_ref[...] * corr + jnp.sum(p, axis=1, keepdims=True)
    acc_ref[...] = acc_ref[...] * corr + jnp.dot(p.astype(BF16), v, preferred_element_type=F32)
    m_ref[...] = m_new


def _dsa_sample_select_body(ksel, g_count, pt_ref, qi_ref, wi_ref, knew_ref, *rest):
    k_refs = rest[:g_count]
    bias_ref, key_ref = rest[g_count:]
    j = pl.program_id(1)
    nj = pl.num_programs(1) - 1
    n = g_count * PAGE_SIZE
    qi = qi_ref[...].astype(BF16)
    wi = wi_ref[...]

    def keys_of(kt):
        o = jnp.dot(qi, kt, preferred_element_type=F32)
        r = jnp.maximum(o, 0.0) * wi
        return _order_key(jnp.sum(r.reshape(8, HI_B, kt.shape[1]), axis=1))

    @pl.when(j < nj)
    def _():
        key_ref[j] = keys_of(jnp.concatenate([r[...] for r in k_refs], 1).astype(BF16))

    @pl.when(j == nj)
    def _():
        new = jnp.where(_new_token_visible((8, PAGE_SIZE)), keys_of(knew_ref[...].astype(BF16)), INT32_MIN)
        key_ref[nj] = jnp.concatenate([new, jnp.full((8, n - PAGE_SIZE), INT32_MIN, jnp.int32)], 1)

        def count_ge(trial):
            def body(c, cnt):
                return cnt + (key_ref[c] >= trial).astype(jnp.int32)

            return jnp.sum(lax.fori_loop(0, nj + 1, body, jnp.zeros((8, n), jnp.int32)), axis=1, keepdims=True)

        def bit_pass(it, prefix):
            trial = prefix | lax.shift_left(jnp.int32(1), 31 - it)
            return jnp.where(count_ge(trial ^ INT32_MIN) >= ksel, trial, prefix)

        prefix = lax.fori_loop(0, 32, bit_pass, jnp.zeros((8, 1), jnp.int32))
        thr = jnp.maximum(prefix ^ INT32_MIN, INT32_MIN + 1)

        def write(c, carry):
            bias_ref[c] = jnp.where(key_ref[c] >= thr, 0.0, MASKED)
            return carry

        lax.fori_loop(0, nj + 1, write, 0)


def dsa_sample_select(page_table, qi8, wi8, ki_new, cache_kidx, g_count=SMP_G):
    bsz, n_pages = page_table.shape
    assert n_pages % g_count == 0
    nj = n_pages // g_count
    n = g_count * PAGE_SIZE
    ksel = min(TOPK_MAX, (n_pages * PAGE_SIZE + SMP_T) // 4)
    grid_spec = pltpu.PrefetchScalarGridSpec(
        num_scalar_prefetch=1,
        grid=(bsz, nj + 1),
        in_specs=[
            pl.BlockSpec((None, 8 * HI_B, DI_B), lambda b, j, pt: (b, 0, 0)),
            pl.BlockSpec((None, 8 * HI_B, 1), lambda b, j, pt: (b, 0, 0)),
            pl.BlockSpec((None, DI_B, PAGE_SIZE), lambda b, j, pt: (b, 0, 0)),
        ] + _page_specs((DI_B, PAGE_SIZE), g_count, nj),
        out_specs=pl.BlockSpec((None, nj + 1, 8, n), lambda b, j, pt: (b, 0, 0, 0)),
        scratch_shapes=[pltpu.VMEM((nj + 1, 8, n), jnp.int32)],
    )
    return pl.pallas_call(
        functools.partial(_dsa_sample_select_body, ksel, g_count),
        grid_spec=grid_spec,
        out_shape=jax.ShapeDtypeStruct((bsz, nj + 1, 8, n), F32),
        compiler_params=_compiler_params(("parallel", "arbitrary")),
        name="dsa_sample_select",
    )(page_table, qi8, wi8, ki_new, *([cache_kidx] * g_count))


def _dsa_sample_attn_body(g_count, pt_ref, q_ref, knew_ref, vnew_ref, bias_ref, *rest):
    k_refs = rest[:g_count]
    v_refs = rest[g_count:2 * g_count]
    o_ref, m_ref, l_ref, acc_ref = rest[2 * g_count:]
    j = pl.program_id(1)
    nj = pl.num_programs(1) - 1
    contract_last = (((1,), (1,)), ((), ()))
    rows_g = SMP_ROWS // KVH_B
    q = q_ref[...].astype(BF16)

    @pl.when(j == 0)
    def _():
        m_ref[...] = jnp.full(m_ref.shape, MASKED, F32)
        l_ref[...] = jnp.zeros(l_ref.shape, F32)
        acc_ref[...] = jnp.zeros(acc_ref.shape, F32)

    def update(ks, vs, bias8):
        s = jnp.concatenate([lax.dot_general(q[g * rows_g:(g + 1) * rows_g], ks[g], contract_last,
                                             preferred_element_type=F32) for g in range(KVH_B)], 0) * ATT_SCALE_B
        s = s + jnp.concatenate([bias8] * (SMP_ROWS // 8), 0)
        m_old = m_ref[...]
        m_new = jnp.maximum(m_old, jnp.max(s, axis=1, keepdims=True))
        p = jnp.exp(s - m_new)
        corr = jnp.exp(m_old - m_new)
        l_ref[...] = l_ref[...] * corr + jnp.sum(p, axis=1, keepdims=True)
        pb = p.astype(BF16)
        pv = jnp.concatenate([jnp.dot(pb[g * rows_g:(g + 1) * rows_g], vs[g], preferred_element_type=F32)
                              for g in range(KVH_B)], 0)
        acc_ref[...] = acc_ref[...] * corr + pv
        m_ref[...] = m_new

    @pl.when(j < nj)
    def _():
        head_rows = lambda r, g: r[pl.ds(g, PAGE_SIZE, stride=KVH_B), :]
        ks = [jnp.concatenate([head_rows(r, g) for r in k_refs], 0).astype(BF16) for g in range(KVH_B)]
        vs = [jnp.concatenate([head_rows(r, g) for r in v_refs], 0).astype(BF16) for g in range(KVH_B)]
        update(ks, vs, bias_ref[...])

    @pl.when(j == nj)
    def _():
        kn = knew_ref[...].astype(BF16)
        vn = vnew_ref[...].astype(BF16)
        update([kn[:, g * HD_B:(g + 1) * HD_B] for g in range(KVH_B)],
               [vn[:, g * HD_B:(g + 1) * HD_B] for g in range(KVH_B)], bias_ref[:, :PAGE_SIZE])
        o_ref[...] = acc_ref[...] / l_ref[...]


def dsa_sample_attn(page_table, q_rows, k_new, v_new, bias, cache_k, cache_v, g_count=SMP_G):
    bsz, n_pages = page_table.shape
    nj = n_pages // g_count
    n = g_count * PAGE_SIZE
    kw = KVH_B * HD_B
    page = (PAGE_SIZE * KVH_B, HD_B)
    grid_spec = pltpu.PrefetchScalarGridSpec(
        num_scalar_prefetch=1,
        grid=(bsz, nj + 1),
        in_specs=[
            pl.BlockSpec((None, SMP_ROWS, HD_B), lambda b, j, pt: (b, 0, 0)),
            pl.BlockSpec((None, PAGE_SIZE, kw), lambda b, j, pt: (b, 0, 0)),
            pl.BlockSpec((None, PAGE_SIZE, kw), lambda b, j, pt: (b, 0, 0)),
            pl.BlockSpec((None, None, 8, n), lambda b, j, pt: (b, j, 0, 0)),
        ] + _page_specs(page, g_count, nj) + _page_specs(page, g_count, nj),
        out_specs=pl.BlockSpec((None, SMP_ROWS, HD_B), lambda b, j, pt: (b, 0, 0)),
        scratch_shapes=[pltpu.VMEM((SMP_ROWS, 1), F32), pltpu.VMEM((SMP_ROWS, 1), F32),
                        pltpu.VMEM((SMP_ROWS, HD_B), F32)],
    )
    return pl.pallas_call(
        functools.partial(_dsa_sample_attn_body, g_count),
        grid_spec=grid_spec,
        out_shape=jax.ShapeDtypeStruct((bsz, SMP_ROWS, HD_B), F32),
        compiler_params=_compiler_params(("parallel", "arbitrary")),
        name="dsa_sample_attn",
    )(page_table, q_rows, k_new, v_new, bias, *([cache_k] * g_count), *([cache_v] * g_count))


def _mla_sample_body(g_count, pt_ref, ql_ref, qr_ref, cnew_ref, rnew_ref, *rest):
    c_refs = rest[:g_count]
    r_refs = rest[g_count:2 * g_count]
    o_ref, m_ref, l_ref, acc_ref = rest[2 * g_count:]
    j = pl.program_id(1)
    nj = pl.num_programs(1) - 1
    contract_last = (((1,), (1,)), ((), ()))
    ql = ql_ref[...].astype(BF16)
    qr = qr_ref[...].astype(BF16)

    @pl.when(j == 0)
    def _():
        m_ref[...] = jnp.full(m_ref.shape, MASKED, F32)
        l_ref[...] = jnp.zeros(l_ref.shape, F32)
        acc_ref[...] = jnp.zeros(acc_ref.shape, F32)

    def update(c, rt, new_tokens):
        s = (lax.dot_general(ql, c, contract_last, preferred_element_type=F32)
             + jnp.dot(qr, rt, preferred_element_type=F32)) * ATT_SCALE_D
        if new_tokens:
            s = jnp.where(_new_token_visible(s.shape), s, MASKED)
        _online_softmax_update(s, c, m_ref, l_ref, acc_ref)

    @pl.when(j < nj)
    def _():
        update(jnp.concatenate([x[...] for x in c_refs], 0).astype(BF16),
               jnp.concatenate([x[...] for x in r_refs], 1).astype(BF16), False)

    @pl.when(j == nj)
    def _():
        update(cnew_ref[...].astype(BF16), rnew_ref[...].astype(BF16), True)
        o_ref[...] = acc_ref[...] / l_ref[...]


def mla_sample_attn(page_table, q_lat, q_rope, c_new, r_new, cache_ckv, cache_krope, g_count=SMP_G):
    bsz, n_pages = page_table.shape
    assert n_pages % g_count == 0
    nj = n_pages // g_count
    grid_spec = pltpu.PrefetchScalarGridSpec(
        num_scalar_prefetch=1,
        grid=(bsz, nj + 1),
        in_specs=[
            pl.BlockSpec((None, SMP_ROWS, R_KV), lambda b, j, pt: (b, 0, 0)),
            pl.BlockSpec((None, SMP_ROWS, DR_D), lambda b, j, pt: (b, 0, 0)),
            pl.BlockSpec((None, PAGE_SIZE, R_KV), lambda b, j, pt: (b, 0, 0)),
            pl.BlockSpec((None, DR_D, PAGE_SIZE), lambda b, j, pt: (b, 0, 0)),
        ] + _page_specs((PAGE_SIZE, R_KV), g_count, nj) + _page_specs((DR_D, PAGE_SIZE), g_count, nj),
        out_specs=pl.BlockSpec((None, SMP_ROWS, R_KV), lambda b, j, pt: (b, 0, 0)),
        scratch_shapes=[pltpu.VMEM((SMP_ROWS, 1), F32), pltpu.VMEM((SMP_ROWS, 1), F32),
                        pltpu.VMEM((SMP_ROWS, R_KV), F32)],
    )
    return pl.pallas_call(
        functools.partial(_mla_sample_body, g_count),
        grid_spec=grid_spec,
        out_shape=jax.ShapeDtypeStruct((bsz, SMP_ROWS, R_KV), F32),
        compiler_params=_compiler_params(("parallel", "arbitrary")),
        name="mla_sample_attn",
    )(page_table, q_lat, q_rope, c_new, r_new, *([cache_ckv] * g_count), *([cache_krope] * g_count))


SCAN_TT = 128
SUBLANE = 8


def _linear_scan_body(a_ref, b_ref, o_ref, h_ref):
    @pl.when(pl.program_id(0) == 0)
    def _():
        h_ref[...] = jnp.zeros_like(h_ref)

    def step(t, h):
        h = a_ref[:, t] * h + b_ref[:, t]
        o_ref[:, t] = h
        return h

    h_ref[...] = lax.fori_loop(0, SCAN_TT, step, h_ref[...], unroll=8)


def linear_scan(a, b):
    bsz, t, c = a.shape
    assert t % SCAN_TT == 0 and c % (SUBLANE * LANE) == 0
    fold = (bsz * (c // (SUBLANE * LANE)), t, SUBLANE, LANE)
    to_tiles = lambda x: jnp.moveaxis(x.reshape(bsz, t, c // (SUBLANE * LANE), SUBLANE, LANE), 2, 1).reshape(fold)
    spec = pl.BlockSpec((fold[0], SCAN_TT, SUBLANE, LANE), lambda i: (0, i, 0, 0))
    h = pl.pallas_call(
        _linear_scan_body,
        grid=(t // SCAN_TT,),
        in_specs=[spec, spec],
        out_specs=spec,
        out_shape=jax.ShapeDtypeStruct(fold, F32),
        scratch_shapes=[pltpu.VMEM((fold[0], SUBLANE, LANE), F32)],
        compiler_params=_compiler_params(("arbitrary",)),
        name="linear_scan",
    )(to_tiles(a), to_tiles(b))
    h = jnp.moveaxis(h.reshape(bsz, c // (SUBLANE * LANE), t, SUBLANE, LANE), 1, 2)
    return h.reshape(bsz, t, c)


PROJ_TM = 256


def _proj_ln_body(x_ref, o_in_ref, w_ref, g_ref, b_ref, o_ref):
    f = jnp.dot(o_in_ref[...].astype(BF16), w_ref[...], preferred_element_type=F32)
    y = DN_ALPHA * x_ref[...] + f
    o_ref[...] = _layer_norm_rows(y, g_ref[...], b_ref[...])


def proj_ln(x, o, w, g, b):
    m, d = x.shape
    k = o.shape[1]
    assert m % PROJ_TM == 0
    return pl.pallas_call(
        _proj_ln_body,
        grid=(m // PROJ_TM,),
        in_specs=[
            pl.BlockSpec((PROJ_TM, d), lambda i: (i, 0)),
            pl.BlockSpec((PROJ_TM, k), lambda i: (i, 0)),
            pl.BlockSpec((k, d), lambda i: (0, 0)),
            pl.BlockSpec((1, d), lambda i: (0, 0)),
            pl.BlockSpec((1, d), lambda i: (0, 0)),
        ],
        out_specs=pl.BlockSpec((PROJ_TM, d), lambda i: (i, 0)),
        out_shape=jax.ShapeDtypeStruct((m, d), F32),
        compiler_params=_compiler_params(("parallel",)),
        name="proj_ln",
    )(x, o, w, g, b)


def _l2_norm(x):
    return x * lax.rsqrt(jnp.sum(x * x, -1, keepdims=True) + RMS_EPS)


def _rms_norm(x, g):
    return x * lax.rsqrt(jnp.mean(x * x, -1, keepdims=True) + RMS_EPS) * g


def _causal_conv(x, buf, w):
    t = x.shape[1]
    xp = jnp.concatenate([buf.astype(x.dtype), x], axis=1)
    y = sum(xp[:, j:j + t] * w[j] for j in range(CONV_W))
    return y, xp[:, t:]


def _rope(x, pos):
    half = DR_D // 2
    inv = ROPE_THETA ** (-jnp.arange(half, dtype=F32) / half)
    ang = pos.astype(F32)[:, None] * inv
    ang = ang.reshape(ang.shape[0], *([1] * (x.ndim - 3)), half)
    cos, sin = jnp.cos(ang), jnp.sin(ang)
    x1, x2 = x[..., :half], x[..., half:]
    return jnp.concatenate([x1 * cos - x2 * sin, x2 * cos + x1 * sin], -1)


def _chunk(a, n, c):
    a = jnp.pad(a, [(0, 0), (0, n * c - a.shape[1])] + [(0, 0)] * (a.ndim - 2))
    a = a.reshape(a.shape[0], n, c, *a.shape[2:])
    return jnp.swapaxes(jnp.swapaxes(a, 0, 1), 2, 3)


def _gated_delta(q, k, v, g, beta, s0):
    bsz, t = q.shape[:2]
    c = min(GDN_CHUNK, t)
    n = -(-t // c)
    qc = _chunk(q * DK_A ** -0.5, n, c)
    kc, vc = _chunk(k, n, c), _chunk(v, n, c)
    gc, bc = _chunk(g, n, c), _chunk(beta, n, c)
    gcum = jnp.cumsum(gc, axis=-1)
    incl = jnp.tril(jnp.ones((c, c), bool))
    strict = jnp.tril(jnp.ones((c, c), bool), -1)
    decay = jnp.exp(jnp.where(incl, gcum[..., :, None] - gcum[..., None, :], -jnp.inf))
    kb = kc * bc[..., None]
    lmat = jnp.where(strict, jnp.einsum('nbhid,nbhjd->nbhij', kb, kc) * decay, 0.0)
    eye = jnp.eye(c, dtype=F32)
    tmat = lax.linalg.triangular_solve(eye + lmat, jnp.broadcast_to(eye, lmat.shape),
                                       left_side=True, lower=True, unit_diagonal=True)
    u = tmat @ (vc * bc[..., None])
    w = tmat @ (kb * jnp.exp(gcum)[..., None])
    qk = jnp.einsum('nbhid,nbhjd->nbhij', qc, kc) * decay
    qg = qc * jnp.exp(gcum)[..., None]
    kd = kc * jnp.exp(gcum[..., -1:] - gcum)[..., None]
    glast = jnp.exp(gcum[..., -1])

    def step(s, xs):
        u_i, w_i, qk_i, qg_i, kd_i, gl_i = xs
        v_new = u_i - jnp.einsum('bhcd,bhde->bhce', w_i, s)
        o = jnp.einsum('bhcd,bhde->bhce', qg_i, s) + jnp.einsum('bhij,bhje->bhie', qk_i, v_new)
        s = s * gl_i[..., None, None] + jnp.einsum('bhcd,bhce->bhde', kd_i, v_new)
        return s, o

    s, o = lax.scan(step, s0, (u, w, qk, qg, kd, glast))
    o = jnp.swapaxes(jnp.swapaxes(o, 2, 3), 0, 1).reshape(bsz, n * c, H_A, DV_A)[:, :t]
    return o, s


def _pad_page(a):
    return jnp.pad(a, ((0, 0), (0, PAGE_SIZE - a.shape[1]), (0, 0)))


def _dsa_sample(q, k, v, qi, ki, wi, cache_k, cache_v, cache_kidx, page_table, g_count=SMP_G):
    bsz, t = q.shape[:2]
    n_grp = H_B // KVH_B
    kw = KVH_B * HD_B
    qi8 = jnp.concatenate([qi, qi], 1).reshape(bsz, 2 * t * HI_B, DI_B)
    wi8 = jnp.concatenate([wi, wi], 1).reshape(bsz, 2 * t * HI_B, 1)
    bias = dsa_sample_select(page_table, qi8, wi8, jnp.swapaxes(_pad_page(ki), 1, 2),
                             jnp.swapaxes(cache_kidx, 1, 2), g_count)
    q_rows = jnp.transpose(q.reshape(bsz, t, KVH_B, n_grp, HD_B), (0, 2, 3, 1, 4)).reshape(bsz, H_B * t, HD_B)
    out = dsa_sample_attn(page_table, q_rows, _pad_page(k.reshape(bsz, t, kw)), _pad_page(v.reshape(bsz, t, kw)), bias,
                          cache_k.reshape(-1, PAGE_SIZE * KVH_B, HD_B), cache_v.reshape(-1, PAGE_SIZE * KVH_B, HD_B),
                          g_count)
    out = out.reshape(bsz, KVH_B, n_grp, t, HD_B)
    return jnp.transpose(out, (0, 3, 1, 2, 4)).reshape(bsz, t, H_B, HD_B)


def _mixer_ab_core(hm, hs, conv_buf, rec_state, sparse_attn, conv_w_a, a_log_a, dt_bias_a, norm_g_a):
    bsz, t, _ = hm.shape
    qkv_pre = hm[..., :A_QKV]
    z = hm[..., A_QKV:A_QKV + 1024]
    q_b = hm[..., 4096:5120]
    qi_b = hm[..., 5120:6144]
    k_b = hm[..., 6144:6400]
    v_b = hm[..., 6400:6656]
    ki_b = hs[..., :DI_B]
    b_gate = hs[..., 64:72]
    a_gate = hs[..., 72:80]
    wi_b = hs[..., 80:96]
    qkv, conv_new = _causal_conv(qkv_pre, conv_buf, conv_w_a)
    qkv = jax.nn.silu(qkv)
    qa = _l2_norm(qkv[..., :1024].reshape(bsz, t, H_A, DK_A))
    ka = _l2_norm(qkv[..., 1024:2048].reshape(bsz, t, H_A, DK_A))
    va = qkv[..., 2048:].reshape(bsz, t, H_A, DV_A)
    beta = jax.nn.sigmoid(b_gate)
    g = -jnp.exp(a_log_a) * jax.nn.softplus(a_gate + dt_bias_a)
    o_a, s_new = _gated_delta(qa, ka, va, g, beta, rec_state)
    o_a = _rms_norm(o_a, norm_g_a) * jax.nn.silu(z.reshape(bsz, t, H_A, DV_A))
    q_b = q_b.reshape(bsz, t, H_B, HD_B)
    k_b = k_b.reshape(bsz, t, KVH_B, HD_B)
    v_b = v_b.reshape(bsz, t, KVH_B, HD_B)
    qi_b = qi_b.reshape(bsz, t, HI_B, DI_B)
    o_b = sparse_attn(q_b, k_b, v_b, qi_b, ki_b, wi_b * IDX_SCALE)
    o = jnp.concatenate([o_a.reshape(bsz, t, -1), o_b.reshape(bsz, t, -1)], -1)
    return o, conv_new, s_new, k_b, v_b, ki_b


def _rglru_branch(xb, yb, conv_buf, h0, conv_w_c, conv_b_c, w_rg_a, b_rg_a, w_rg_x, b_rg_x, lambda_c):
    bsz, t, _ = xb.shape
    xc, conv_new = _causal_conv(xb, conv_buf, conv_w_c)
    xc = xc + conv_b_c
    xr = xc.reshape(bsz, t, NB_C, BW_C)
    r = jax.nn.sigmoid(jnp.einsum('btnd,nde->btne', xr, w_rg_a).reshape(bsz, t, D_C) + b_rg_a)
    i = jax.nn.sigmoid(jnp.einsum('btnd,nde->btne', xr, w_rg_x).reshape(bsz, t, D_C) + b_rg_x)
    log_a = -RG_C * r * jax.nn.softplus(-lambda_c)
    a = jnp.exp(log_a)
    bterm = jnp.sqrt(-jnp.expm1(2.0 * log_a)) * (i * xc)
    bterm = bterm.at[:, 0].add(a[:, 0] * h0)

    if t % SCAN_TT == 0:
        h = linear_scan(a, bterm)
    else:
        def comb(lhs, rhs):
            return lhs[0] * rhs[0], rhs[0] * lhs[1] + rhs[1]

        _, h = lax.associative_scan(comb, (a, bterm), axis=1)
    return h * jax.nn.gelu(yb), conv_new, h[:, -1]


def _mla_prompt(qn, qr, ckv, kr, w_uk, w_uv):
    bsz, s_len = qn.shape[:2]
    m = bsz * s_len
    ckv2 = ckv.reshape(m, R_KV)
    k_nope = matmul(ckv2, w_uk.reshape(R_KV, H_D * DN_D).astype(BF16), H_D * DN_D, BF16)
    v = matmul(ckv2, w_uv.reshape(R_KV, H_D * DV_D).astype(BF16), H_D * DV_D, BF16)
    qr_hm = jnp.moveaxis(qr.reshape(m, H_D, DR_D), 1, 0)
    out = mla_prompt(qn.reshape(m, H_D * DN_D), qr_hm, k_nope, kr.reshape(m, DR_D).astype(BF16), v, bsz, s_len)
    return out.reshape(bsz, s_len, H_D, DV_D)


def _mla_sample(qn, qr, ckv, kr, w_uk, w_uv, cache_ckv, cache_krope, page_table, g_count=SMP_G):
    bsz, t = qn.shape[:2]
    q_lat = jnp.einsum('bthd,rhd->bthr', qn, w_uk)
    rows = lambda a: jnp.swapaxes(a, 1, 2).reshape(bsz, H_D * t, a.shape[-1])
    lat = mla_sample_attn(page_table, rows(q_lat), rows(qr), _pad_page(ckv), jnp.swapaxes(_pad_page(kr), 1, 2),
                          cache_ckv, jnp.swapaxes(cache_krope, 1, 2), g_count)
    lat = jnp.swapaxes(lat.reshape(bsz, H_D, t, R_KV), 1, 2)
    return jnp.einsum('bthr,rhd->bthd', lat, w_uv)


def _mixer_cd_core(hm, hs, conv_buf, h0, pos0, latent_attn, conv_w_c, conv_b_c, w_rg_a, b_rg_a,
                   w_rg_x, b_rg_x, lambda_c, kv_norm_g_d):
    bsz, t, _ = hm.shape
    xb = hm[..., :D_C]
    yb = hm[..., D_C:2 * D_C]
    qn = hm[..., 2048:3072].reshape(bsz, t, H_D, DN_D)
    qr = hm[..., 3072:3584].reshape(bsz, t, H_D, DR_D)
    ckv = hm[..., 3584:3840]
    kr = hs[..., :DR_D]
    o_c, conv_new, h_new = _rglru_branch(xb, yb, conv_buf, h0, conv_w_c, conv_b_c, w_rg_a, b_rg_a,
                                         w_rg_x, b_rg_x, lambda_c)
    pos = pos0 + jnp.arange(t)
    qr = _rope(qr, pos)
    kr = _rope(kr, pos)
    ckv = _rms_norm(ckv, kv_norm_g_d)
    o_d = latent_attn(qn, qr, ckv, kr)
    o = jnp.concatenate([o_c, o_d.reshape(bsz, t, -1)], -1)
    return o, conv_new, h_new, ckv, kr


def _pad_cols(w, n):
    return jnp.pad(w, ((0, 0), (0, n - w.shape[1])))


def kernel(x_prompt, x_sample, state_a_conv, state_a_rec, cache_b_k, cache_b_v, cache_b_kidx, state_c_conv, state_c_h, cache_d_ckv, cache_d_krope, page_table, ln_g, ln_b, ffn_w_gate, ffn_w_up, ffn_w_down, w_in_ab, conv_w_a, a_log_a, dt_bias_a, norm_g_a, w_out_ab, w_in_cd, conv_w_c, conv_b_c, w_rg_a, b_rg_a, w_rg_x, b_rg_x, lambda_c, kv_norm_g_d, w_uk_d, w_uv_d, w_out_cd):
    bp, tp = x_prompt.shape[:2]
    bs, ts = x_sample.shape[:2]
    mp = bp * tp
    past = page_table.shape[1] * PAGE_SIZE
    n_pg = tp // PAGE_SIZE

    x = jnp.concatenate([x_prompt.reshape(mp, D_MODEL), x_sample.reshape(bs * ts, D_MODEL)], 0)

    wg = ffn_w_gate.astype(BF16)
    wu = ffn_w_up.astype(BF16)
    wd = ffn_w_down.astype(BF16)

    ab = np.cumsum((0, A_QKV, H_A * DV_A, H_A, H_A, H_B * HD_B, KVH_B * HD_B, KVH_B * HD_B, HI_B * DI_B, DI_B, HI_B))
    col = lambda w, i: w[:, int(ab[i]):int(ab[i + 1])]
    w_ab_main = jnp.concatenate([col(w_in_ab, 0), col(w_in_ab, 1), col(w_in_ab, 4), col(w_in_ab, 7),
                                 col(w_in_ab, 5), col(w_in_ab, 6)], 1).astype(BF16)
    w_ab_small = _pad_cols(jnp.concatenate([col(w_in_ab, 8), col(w_in_ab, 2), col(w_in_ab, 3), col(w_in_ab, 9)], 1),
                           LANE).astype(BF16)
    w_q_d = w_in_cd[:, 2 * D_C:2 * D_C + H_D * (DN_D + DR_D)].reshape(D_MODEL, H_D, DN_D + DR_D)
    w_cd_main = jnp.concatenate([w_in_cd[:, :2 * D_C],
                                 w_q_d[..., :DN_D].reshape(D_MODEL, H_D * DN_D),
                                 w_q_d[..., DN_D:].reshape(D_MODEL, H_D * DR_D),
                                 w_in_cd[:, 3584:3840]], 1).astype(BF16)
    w_cd_small = _pad_cols(w_in_cd[:, 3840:], LANE).astype(BF16)
    w_out_ab_b = w_out_ab.astype(BF16)
    w_out_cd_b = w_out_cd.astype(BF16)

    def ffn(x, layer, half, ln_idx):
        return ffn_ln(x, wg[layer, half], wu[layer, half], wd[layer, half],
                      ln_g[layer, ln_idx][None], ln_b[layer, ln_idx][None])

    x = ffn(x, 0, 0, 0)
    hm = matmul(x, w_ab_main, 512)
    hs = matmul(x, w_ab_small, LANE)
    o_p, a_conv_p, a_rec_p, kb_p, vb_p, kib_p = _mixer_ab_core(
        hm[:mp].reshape(bp, tp, -1), hs[:mp].reshape(bp, tp, -1),
        jnp.zeros((bp, CONV_W - 1, A_QKV), F32), jnp.zeros((bp, H_A, DK_A, DV_A), F32),
        lambda *unused: dsa_prompt(hm, hs, bp, tp), conv_w_a, a_log_a, dt_bias_a, norm_g_a)
    attn_s = functools.partial(_dsa_sample, cache_k=cache_b_k, cache_v=cache_b_v,
                               cache_kidx=cache_b_kidx, page_table=page_table)
    o_s, a_conv_s, a_rec_s, b_k_s, b_v_s, b_kidx_s = _mixer_ab_core(
        hm[mp:].reshape(bs, ts, -1), hs[mp:].reshape(bs, ts, -1), state_a_conv, state_a_rec,
        attn_s, conv_w_a, a_log_a, dt_bias_a, norm_g_a)
    b_k_p = kb_p.reshape(bp, n_pg, PAGE_SIZE, KVH_B, HD_B)
    b_v_p = vb_p.reshape(bp, n_pg, PAGE_SIZE, KVH_B, HD_B)
    b_kidx_p = kib_p.reshape(bp, n_pg, PAGE_SIZE, DI_B)
    o = jnp.concatenate([o_p.reshape(mp, -1), o_s.reshape(bs * ts, -1)], 0)
    x = proj_ln(x, o, w_out_ab_b, ln_g[0, 1][None], ln_b[0, 1][None])
    x = ffn(x, 0, 1, 2)

    x = ffn(x, 1, 0, 0)
    hm = matmul(x, w_cd_main, 768)
    hs = matmul(x, w_cd_small, LANE)
    attn_p = functools.partial(_mla_prompt, w_uk=w_uk_d, w_uv=w_uv_d)
    o_p, c_conv_p, c_h_p, ckv_p, kr_p = _mixer_cd_core(
        hm[:mp].reshape(bp, tp, -1), hs[:mp].reshape(bp, tp, -1),
        jnp.zeros((bp, CONV_W - 1, D_C), F32), jnp.zeros((bp, D_C), F32), 0, attn_p,
        conv_w_c, conv_b_c, w_rg_a, b_rg_a, w_rg_x, b_rg_x, lambda_c, kv_norm_g_d)
    attn_s = functools.partial(_mla_sample, w_uk=w_uk_d, w_uv=w_uv_d, cache_ckv=cache_d_ckv,
                               cache_krope=cache_d_krope, page_table=page_table)
    o_s, c_conv_s, c_h_s, d_ckv_s, d_krope_s = _mixer_cd_core(
        hm[mp:].reshape(bs, ts, -1), hs[mp:].reshape(bs, ts, -1), state_c_conv, state_c_h, past, attn_s,
        conv_w_c, conv_b_c, w_rg_a, b_rg_a, w_rg_x, b_rg_x, lambda_c, kv_norm_g_d)
    d_ckv_p = ckv_p.reshape(bp, n_pg, PAGE_SIZE, R_KV)
    d_krope_p = kr_p.reshape(bp, n_pg, PAGE_SIZE, DR_D)
    o = jnp.concatenate([o_p.reshape(mp, -1), o_s.reshape(bs * ts, -1)], 0)
    x = proj_ln(x, o, w_out_cd_b, ln_g[1, 1][None], ln_b[1, 1][None])
    x = ffn(x, 1, 1, 2)

    yp = x[:mp].reshape(bp, tp, D_MODEL)
    ys = x[mp:].reshape(bs, ts, D_MODEL)
    return (yp, ys,
            a_conv_p, a_conv_s, a_rec_p, a_rec_s,
            b_k_p, b_k_s, b_v_p, b_v_s, b_kidx_p, b_kidx_s,
            c_conv_p, c_conv_s, c_h_p, c_h_s,
            d_ckv_p, d_ckv_s, d_krope_p, d_krope_s)
```

```python
import functools

import jax
import jax.numpy as jnp
import numpy as np
from jax import lax
from jax.experimental import pallas as pl
from jax.experimental.pallas import tpu as pltpu

D_MODEL = 2048
DEPTH = 2
PAGE_SIZE = 128
H_A = 8
DK_A = 128
DV_A = 128
CONV_W = 4
GDN_CHUNK = 64
A_QKV = 2 * H_A * DK_A + H_A * DV_A
H_B = 8
KVH_B = 2
HD_B = 128
HI_B = 16
DI_B = 64
TOPK_MAX = 256
D_C = 1024
NB_C = 8
BW_C = D_C // NB_C
RG_C = 8.0
H_D = 8
DN_D = 128
DR_D = 64
DV_D = 128
R_KV = 256
ROPE_THETA = 10000.0
D_FF = 5632
Q_BLOCK = 128
LN_EPS = 1e-5
RMS_EPS = 1e-6
DN_ALPHA = (2 * DEPTH) ** 0.25
IDX_SCALE = (HI_B * DI_B) ** -0.5
ATT_SCALE_B = HD_B ** -0.5
ATT_SCALE_D = (DN_D + DR_D) ** -0.5
F32 = jnp.float32
BF16 = jnp.bfloat16

VMEM_LIMIT_BYTES = 56 * 1024 * 1024
LANE = 128


def _compiler_params(semantics):
    return pltpu.CompilerParams(dimension_semantics=semantics, vmem_limit_bytes=VMEM_LIMIT_BYTES)


def _layer_norm_rows(y, g, b):
    mu = jnp.mean(y, axis=-1, keepdims=True)
    d = y - mu
    var = jnp.mean(d * d, axis=-1, keepdims=True)
    return d * lax.rsqrt(var + LN_EPS) * g + b


FFN_TM = 512
FFN_TF = 512


def _ffn_ln_body(x_ref, wg_ref, wu_ref, wd_ref, g_ref, b_ref, o_ref, xb_ref, acc_ref):
    j = pl.program_id(1)

    @pl.when(j == 0)
    def _():
        xb_ref[...] = x_ref[...].astype(BF16)
        acc_ref[...] = jnp.zeros_like(acc_ref)

    xb = xb_ref[...]
    hg = jnp.dot(xb, wg_ref[...], preferred_element_type=F32)
    hu = jnp.dot(xb, wu_ref[...], preferred_element_type=F32)
    h = (hg * jax.nn.sigmoid(hg)) * hu
    acc_ref[...] += jnp.dot(h.astype(BF16), wd_ref[...], preferred_element_type=F32)

    @pl.when(j == pl.num_programs(1) - 1)
    def _():
        y = DN_ALPHA * x_ref[...] + 0.5 * acc_ref[...]
        o_ref[...] = _layer_norm_rows(y, g_ref[...], b_ref[...])


def ffn_ln(x, wg, wu, wd, g, b):
    m, d = x.shape
    f = wg.shape[1]
    assert m % FFN_TM == 0 and f % FFN_TF == 0
    return pl.pallas_call(
        _ffn_ln_body,
        grid=(m // FFN_TM, f // FFN_TF),
        in_specs=[
            pl.BlockSpec((FFN_TM, d), lambda i, j: (i, 0)),
            pl.BlockSpec((d, FFN_TF), lambda i, j: (0, j)),
            pl.BlockSpec((d, FFN_TF), lambda i, j: (0, j)),
            pl.BlockSpec((FFN_TF, d), lambda i, j: (j, 0)),
            pl.BlockSpec((1, d), lambda i, j: (0, 0)),
            pl.BlockSpec((1, d), lambda i, j: (0, 0)),
        ],
        out_specs=pl.BlockSpec((FFN_TM, d), lambda i, j: (i, 0)),
        out_shape=jax.ShapeDtypeStruct((m, d), F32),
        scratch_shapes=[pltpu.VMEM((FFN_TM, d), BF16), pltpu.VMEM((FFN_TM, d), F32)],
        compiler_params=_compiler_params(("parallel", "arbitrary")),
        name="ffn_ln",
    )(x, wg, wu, wd, g, b)


MM_TM = 512


def _matmul_body(x_ref, w_ref, o_ref, xb_ref):
    @pl.when(pl.program_id(1) == 0)
    def _():
        xb_ref[...] = x_ref[...].astype(BF16)

    o_ref[...] = jnp.dot(xb_ref[...], w_ref[...], preferred_element_type=F32).astype(o_ref.dtype)


def matmul(x, w, tn, out_dtype=F32):
    m, k = x.shape
    n = w.shape[1]
    assert m % MM_TM == 0 and n % tn == 0
    return pl.pallas_call(
        _matmul_body,
        grid=(m // MM_TM, n // tn),
        in_specs=[
            pl.BlockSpec((MM_TM, k), lambda i, j: (i, 0)),
            pl.BlockSpec((k, tn), lambda i, j: (0, j)),
        ],
        out_specs=pl.BlockSpec((MM_TM, tn), lambda i, j: (i, j)),
        out_shape=jax.ShapeDtypeStruct((m, n), out_dtype),
        scratch_shapes=[pltpu.VMEM((MM_TM, k), BF16)],
        compiler_params=_compiler_params(("parallel", "arbitrary")),
        name="matmul",
    )(x, w)


DSA_TQ = 128
DSA_KC = 512
MASKED = -1e30
INT32_MIN = -2 ** 31
AB_Q_COL, AB_QI_COL, AB_K_COL, AB_V_COL = 4096, 5120, 6144, 6400
AB_WI_LANE = 80


def _order_key(x):
    bits = lax.bitcast_convert_type(jnp.where(x == 0.0, 0.0, x), jnp.int32)
    return jnp.where(bits < 0, bits ^ 0x7FFFFFFF, bits)


def _dsa_prompt_body(ksel, q_ref, qi_ref, hsq_ref, k_ref, v_ref, hsk_ref, o_ref,
                     kb_ref, vb_ref, kk_ref, keyt_ref, bias_ref, m_ref, l_ref, acc_ref):
    i = pl.program_id(1)
    tq, kc = DSA_TQ, DSA_KC
    n_all = kk_ref.shape[0] // kc
    n_grp = H_B // KVH_B
    contract_last = (((1,), (1,)), ((), ()))

    @pl.when(i == 0)
    def _():
        def cast_chunk(c, carry):
            rows = pl.ds(pl.multiple_of(c * kc, kc), kc)
            kf = k_ref[rows, :]
            vf = v_ref[rows, :]
            for g in range(KVH_B):
                kb_ref[g, rows, :] = kf[:, g * HD_B:(g + 1) * HD_B].astype(BF16)
                vb_ref[g, rows, :] = vf[:, g * HD_B:(g + 1) * HD_B].astype(BF16)
            hs = hsk_ref[rows, :]
            lane = lax.broadcasted_iota(jnp.int32, hs.shape, 1)
            kk_ref[rows, :] = jnp.where(lane < DI_B, hs, pltpu.roll(hs, DI_B, 1)).astype(BF16)
            return carry

        lax.fori_loop(0, n_all, cast_chunk, 0)

    nk = (i * tq + tq + kc - 1) // kc

    qi = qi_ref[...]
    lane = lax.broadcasted_iota(jnp.int32, (tq, 2 * DI_B), 1)
    rhs = []
    for j in range(HI_B // 2):
        pair = qi[:, j * 2 * DI_B:(j + 1) * 2 * DI_B]
        rhs.append(jnp.concatenate([jnp.where(lane < DI_B, pair, 0.0),
                                    jnp.where(lane >= DI_B, pair, 0.0)], 0).astype(BF16))
    wi_t = jnp.transpose(hsq_ref[...])[AB_WI_LANE:AB_WI_LANE + HI_B, :] * IDX_SCALE
    t_pos = i * tq + lax.broadcasted_iota(jnp.int32, (kc, tq), 1)

    def score_chunk(c, carry):
        kk = kk_ref[pl.ds(pl.multiple_of(c * kc, kc), kc), :]
        sc = jnp.zeros((kc, tq), F32)
        for j in range(HI_B // 2):
            o = lax.dot_general(kk, rhs[j], contract_last, preferred_element_type=F32)
            sc = sc + wi_t[2 * j:2 * j + 1, :] * jnp.maximum(o[:, :tq], 0.0)
            sc = sc + wi_t[2 * j + 1:2 * j + 2, :] * jnp.maximum(o[:, tq:], 0.0)
        s_pos = c * kc + lax.broadcasted_iota(jnp.int32, (kc, tq), 0)
        keyt_ref[c] = jnp.where(s_pos <= t_pos, _order_key(sc), INT32_MIN)
        return carry

    lax.fori_loop(0, nk, score_chunk, 0)

    def count_ge(trial):
        def body(c, cnt):
            ge = (keyt_ref[c] >= trial).astype(jnp.int32)
            return cnt + jnp.sum(ge.reshape(kc // 8, 8, tq), axis=0)

        cnt = lax.fori_loop(0, nk, body, jnp.zeros((8, tq), jnp.int32))
        return jnp.sum(cnt, axis=0, keepdims=True)

    def bit_pass(it, prefix):
        trial = prefix | lax.shift_left(jnp.int32(1), 31 - it)
        return jnp.where(count_ge(trial ^ INT32_MIN) >= ksel, trial, prefix)

    prefix = lax.fori_loop(0, 32, bit_pass, jnp.zeros((1, tq), jnp.int32))
    thr = jnp.maximum(prefix ^ INT32_MIN, INT32_MIN + 1)

    def bias_chunk(c, carry):
        bias_ref[c] = jnp.transpose(jnp.where(keyt_ref[c] >= thr, 0.0, MASKED))
        return carry

    lax.fori_loop(0, nk, bias_chunk, 0)

    q = q_ref[...]
    qg = [jnp.concatenate([q[:, (g * n_grp + hh) * HD_B:(g * n_grp + hh + 1) * HD_B] for hh in range(n_grp)],
                          0).astype(BF16) for g in range(KVH_B)]
    m_ref[...] = jnp.full(m_ref.shape, MASKED, F32)
    l_ref[...] = jnp.zeros(l_ref.shape, F32)
    acc_ref[...] = jnp.zeros(acc_ref.shape, F32)

    def attn_chunk(c, carry):
        rows = pl.ds(pl.multiple_of(c * kc, kc), kc)
        bias = bias_ref[c]
        for g in range(KVH_B):
            s = lax.dot_general(qg[g], kb_ref[g, rows, :], contract_last, preferred_element_type=F32) * ATT_SCALE_B
            s = (s.reshape(n_grp, tq, kc) + bias[None]).reshape(n_grp * tq, kc)
            m_old = m_ref[g]
            m_new = jnp.maximum(m_old, jnp.max(s, axis=1, keepdims=True))
            p = jnp.exp(s - m_new)
            corr = jnp.exp(m_old - m_new)
            l_ref[g] = l_ref[g] * corr + jnp.sum(p, axis=1, keepdims=True)
            acc_ref[g] = acc_ref[g] * corr + jnp.dot(p.astype(BF16), vb_ref[g, rows, :], preferred_element_type=F32)
            m_ref[g] = m_new
        return carry

    lax.fori_loop(0, nk, attn_chunk, 0)

    for g in range(KVH_B):
        o = acc_ref[g] / l_ref[g]
        for hh in range(n_grp):
            o_ref[:, (g * n_grp + hh) * HD_B:(g * n_grp + hh + 1) * HD_B] = o[hh * tq:(hh + 1) * tq, :]


def dsa_prompt(hm, hs, bsz, t):
    tq, kc = DSA_TQ, DSA_KC
    assert t % kc == 0 and kc % tq == 0
    nq = t // tq
    ksel = min(TOPK_MAX, t // 4)
    qw = H_B * HD_B
    kw = KVH_B * HD_B
    n_grp = H_B // KVH_B
    return pl.pallas_call(
        functools.partial(_dsa_prompt_body, ksel),
        grid=(bsz, nq),
        in_specs=[
            pl.BlockSpec((tq, qw), lambda b, i: (b * nq + i, AB_Q_COL // qw)),
            pl.BlockSpec((tq, HI_B * DI_B), lambda b, i: (b * nq + i, AB_QI_COL // (HI_B * DI_B))),
            pl.BlockSpec((tq, LANE), lambda b, i: (b * nq + i, 0)),
            pl.BlockSpec((t, kw), lambda b, i: (b, AB_K_COL // kw)),
            pl.BlockSpec((t, kw), lambda b, i: (b, AB_V_COL // kw)),
            pl.BlockSpec((t, LANE), lambda b, i: (b, 0)),
        ],
        out_specs=pl.BlockSpec((tq, qw), lambda b, i: (b * nq + i, 0)),
        out_shape=jax.ShapeDtypeStruct((bsz * t, qw), F32),
        scratch_shapes=[
            pltpu.VMEM((KVH_B, t, HD_B), BF16),
            pltpu.VMEM((KVH_B, t, HD_B), BF16),
            pltpu.VMEM((t, LANE), BF16),
            pltpu.VMEM((t // kc, kc, tq), jnp.int32),
            pltpu.VMEM((t // kc, tq, kc), F32),
            pltpu.VMEM((KVH_B, n_grp * tq, 1), F32),
            pltpu.VMEM((KVH_B, n_grp * tq, 1), F32),
            pltpu.VMEM((KVH_B, n_grp * tq, HD_B), F32),
        ],
        compiler_params=_compiler_params(("parallel", "arbitrary")),
        name="dsa_prompt",
    )(hm, hm, hs, hm, hm, hs)


MLA_TQ = 512


def _mla_prompt_body(qn_ref, qr_ref, kn_ref, kr_ref, v_ref, o_ref, kcat_ref, m_ref, l_ref, acc_ref):
    i = pl.program_id(2)
    tq = MLA_TQ
    contract_last = (((1,), (1,)), ((), ()))
    pad = kcat_ref.shape[1] - DN_D - DR_D

    @pl.when(i == 0)
    def _():
        def build(c, carry):
            rows = pl.ds(pl.multiple_of(c * tq, tq), tq)
            kcat_ref[rows, :] = jnp.concatenate([kn_ref[rows, :], kr_ref[rows, :], jnp.zeros((tq, pad), BF16)], 1)
            return carry

        lax.fori_loop(0, kcat_ref.shape[0] // tq, build, 0)

    q = jnp.concatenate([qn_ref[...], qr_ref[...], jnp.zeros((tq, pad), F32)], 1).astype(BF16)
    m_ref[...] = jnp.full(m_ref.shape, MASKED, F32)
    l_ref[...] = jnp.zeros(l_ref.shape, F32)
    acc_ref[...] = jnp.zeros(acc_ref.shape, F32)

    def chunk(c, diagonal):
        rows = pl.ds(pl.multiple_of(c * tq, tq), tq)
        s = lax.dot_general(q, kcat_ref[rows, :], contract_last, preferred_element_type=F32) * ATT_SCALE_D
        if diagonal:
            causal = (lax.broadcasted_iota(jnp.int32, (tq, tq), 1) <= lax.broadcasted_iota(jnp.int32, (tq, tq), 0))
            s = jnp.where(causal, s, MASKED)
        m_old = m_ref[...]
        m_new = jnp.maximum(m_old, jnp.max(s, axis=1, keepdims=True))
        p = jnp.exp(s - m_new)
        corr = jnp.exp(m_old - m_new)
        l_ref[...] = l_ref[...] * corr + jnp.sum(p, axis=1, keepdims=True)
        acc_ref[...] = acc_ref[...] * corr + jnp.dot(p.astype(BF16), v_ref[rows, :], preferred_element_type=F32)
        m_ref[...] = m_new

    def below_diagonal(c, carry):
        chunk(c, False)
        return carry

    lax.fori_loop(0, i, below_diagonal, 0)
    chunk(i, True)
    o_ref[...] = acc_ref[...] / l_ref[...]


def mla_prompt(qn, qr, kn, kr, v, bsz, t):
    tq = MLA_TQ
    assert t % tq == 0
    nq = t // tq
    return pl.pallas_call(
        _mla_prompt_body,
        grid=(bsz, H_D, nq),
        in_specs=[
            pl.BlockSpec((tq, DN_D), lambda b, h, i: (b * nq + i, h)),
            pl.BlockSpec((None, tq, DR_D), lambda b, h, i: (h, b * nq + i, 0)),
            pl.BlockSpec((t, DN_D), lambda b, h, i: (b, h)),
            pl.BlockSpec((t, DR_D), lambda b, h, i: (b, 0)),
            pl.BlockSpec((t, DV_D), lambda b, h, i: (b, h)),
        ],
        out_specs=pl.BlockSpec((tq, DV_D), lambda b, h, i: (b * nq + i, h)),
        out_shape=jax.ShapeDtypeStruct((bsz * t, H_D * DV_D), F32),
        scratch_shapes=[
            pltpu.VMEM((t, 2 * LANE), BF16),
            pltpu.VMEM((tq, 1), F32),
            pltpu.VMEM((tq, 1), F32),
            pltpu.VMEM((tq, DV_D), F32),
        ],
        compiler_params=_compiler_params(("parallel", "parallel", "arbitrary")),
        name="mla_prompt",
    )(qn, qr, kn, kr, v)


SMP_G = 16
SMP_ROWS = 32
SMP_T = 4


def _page_specs(page_shape, g_count, nj):
    zeros = (0,) * len(page_shape)

    def make(g):
        return pl.BlockSpec((None,) + page_shape,
                            lambda b, j, pt: (pt[b, jnp.minimum(j, nj - 1) * g_count + g],) + zeros)

    return [make(g) for g in range(g_count)]


def _new_token_visible(shape):
    token = lax.broadcasted_iota(jnp.int32, shape, 0) & (SMP_T - 1)
    return lax.broadcasted_iota(jnp.int32, shape, 1) <= token


def _online_softmax_update(s, v, m_ref, l_ref, acc_ref):
    m_old = m_ref[...]
    m_new = jnp.maximum(m_old, jnp.max(s, axis=1, keepdims=True))
    p = jnp.exp(s - m_new)
    corr = jnp.exp(m_old - m_new)
    l_ref[...] = l_ref[...] * corr + jnp.sum(p, axis=1, keepdims=True)
    acc_ref[...] = acc_ref[...] * corr + jnp.dot(p.astype(BF16), v, preferred_element_type=F32)
    m_ref[...] = m_new


def _dsa_sample_select_body(ksel, g_count, pt_ref, qi_ref, wi_ref, knew_ref, *rest):
    k_refs = rest[:g_count]
    bias_ref, key_ref = rest[g_count:]
    j = pl.program_id(1)
    nj = pl.num_programs(1) - 1
    n = g_count * PAGE_SIZE
    qi = qi_ref[...].astype(BF16)
    wi = wi_ref[...]

    def keys_of(kt):
        o = jnp.dot(qi, kt, preferred_element_type=F32)
        r = jnp.maximum(o, 0.0) * wi
        return _order_key(jnp.sum(r.reshape(8, HI_B, kt.shape[1]), axis=1))

    @pl.when(j < nj)
    def _():
        key_ref[j] = keys_of(jnp.concatenate([r[...] for r in k_refs], 1).astype(BF16))

    @pl.when(j == nj)
    def _():
        new = jnp.where(_new_token_visible((8, PAGE_SIZE)), keys_of(knew_ref[...].astype(BF16)), INT32_MIN)
        key_ref[nj] = jnp.concatenate([new, jnp.full((8, n - PAGE_SIZE), INT32_MIN, jnp.int32)], 1)

        def count_ge(trial):
            def body(c, cnt):
                return cnt + (key_ref[c] >= trial).astype(jnp.int32)

            cnt = lax.fori_loop(0, key_ref.shape[0], body, jnp.zeros((8, n), jnp.int32), unroll=True)
            return jnp.sum(cnt, axis=1, keepdims=True)

        def bit_pass(it, prefix):
            trial = prefix | lax.shift_left(jnp.int32(1), 31 - it)
            return jnp.where(count_ge(trial ^ INT32_MIN) >= ksel, trial, prefix)

        prefix = lax.fori_loop(0, 32, bit_pass, jnp.zeros((8, 1), jnp.int32))
        thr = jnp.maximum(prefix ^ INT32_MIN, INT32_MIN + 1)

        def write(c, carry):
            bias_ref[c] = jnp.where(key_ref[c] >= thr, 0.0, MASKED)
            return carry

        lax.fori_loop(0, nj + 1, write, 0)


def dsa_sample_select(page_table, qi8, wi8, ki_new, cache_kidx, g_count=SMP_G):
    bsz, n_pages = page_table.shape
    assert n_pages % g_count == 0
    nj = n_pages // g_count
    n = g_count * PAGE_SIZE
    ksel = min(TOPK_MAX, (n_pages * PAGE_SIZE + SMP_T) // 4)
    grid_spec = pltpu.PrefetchScalarGridSpec(
        num_scalar_prefetch=1,
        grid=(bsz, nj + 1),
        in_specs=[
            pl.BlockSpec((None, 8 * HI_B, DI_B), lambda b, j, pt: (b, 0, 0)),
            pl.BlockSpec((None, 8 * HI_B, 1), lambda b, j, pt: (b, 0, 0)),
            pl.BlockSpec((None, DI_B, PAGE_SIZE), lambda b, j, pt: (b, 0, 0)),
        ] + _page_specs((DI_B, PAGE_SIZE), g_count, nj),
        out_specs=pl.BlockSpec((None, nj + 1, 8, n), lambda b, j, pt: (b, 0, 0, 0)),
        scratch_shapes=[pltpu.VMEM((nj + 1, 8, n), jnp.int32)],
    )
    return pl.pallas_call(
        functools.partial(_dsa_sample_select_body, ksel, g_count),
        grid_spec=grid_spec,
        out_shape=jax.ShapeDtypeStruct((bsz, nj + 1, 8, n), F32),
        compiler_params=_compiler_params(("parallel", "arbitrary")),
        name="dsa_sample_select",
    )(page_table, qi8, wi8, ki_new, *([cache_kidx] * g_count))


def _dsa_sample_attn_body(g_count, pt_ref, q_ref, knew_ref, vnew_ref, bias_ref, *rest):
    k_refs = rest[:g_count]
    v_refs = rest[g_count:2 * g_count]
    o_ref, m_ref, l_ref, acc_ref = rest[2 * g_count:]
    j = pl.program_id(1)
    nj = pl.num_programs(1) - 1
    contract_last = (((1,), (1,)), ((), ()))
    rows_g = SMP_ROWS // KVH_B
    q = q_ref[...].astype(BF16)

    @pl.when(j == 0)
    def _():
        m_ref[...] = jnp.full(m_ref.shape, MASKED, F32)
        l_ref[...] = jnp.zeros(l_ref.shape, F32)
        acc_ref[...] = jnp.zeros(acc_ref.shape, F32)

    def update(ks, vs, bias8):
        s = jnp.concatenate([lax.dot_general(q[g * rows_g:(g + 1) * rows_g], ks[g], contract_last,
                                             preferred_element_type=F32) for g in range(KVH_B)], 0) * ATT_SCALE_B
        s = s + jnp.concatenate([bias8] * (SMP_ROWS // 8), 0)
        m_old = m_ref[...]
        m_new = jnp.maximum(m_old, jnp.max(s, axis=1, keepdims=True))
        p = jnp.exp(s - m_new)
        corr = jnp.exp(m_old - m_new)
        l_ref[...] = l_ref[...] * corr + jnp.sum(p, axis=1, keepdims=True)
        pb = p.astype(BF16)
        pv = jnp.concatenate([jnp.dot(pb[g * rows_g:(g + 1) * rows_g], vs[g], preferred_element_type=F32)
                              for g in range(KVH_B)], 0)
        acc_ref[...] = acc_ref[...] * corr + pv
        m_ref[...] = m_new

    @pl.when(j < nj)
    def _():
        head_rows = lambda r, g: r[pl.ds(g, PAGE_SIZE, stride=KVH_B), :]
        ks = [jnp.concatenate([head_rows(r, g) for r in k_refs], 0).astype(BF16) for g in range(KVH_B)]
        vs = [jnp.concatenate([head_rows(r, g) for r in v_refs], 0).astype(BF16) for g in range(KVH_B)]
        update(ks, vs, bias_ref[...])

    @pl.when(j == nj)
    def _():
        kn = knew_ref[...].astype(BF16)
        vn = vnew_ref[...].astype(BF16)
        update([kn[:, g * HD_B:(g + 1) * HD_B] for g in range(KVH_B)],
               [vn[:, g * HD_B:(g + 1) * HD_B] for g in range(KVH_B)], bias_ref[:, :PAGE_SIZE])
        o_ref[...] = acc_ref[...] / l_ref[...]


def dsa_sample_attn(page_table, q_rows, k_new, v_new, bias, cache_k, cache_v, g_count=SMP_G):
    bsz, n_pages = page_table.shape
    nj = n_pages // g_count
    n = g_count * PAGE_SIZE
    kw = KVH_B * HD_B
    page = (PAGE_SIZE * KVH_B, HD_B)
    grid_spec = pltpu.PrefetchScalarGridSpec(
        num_scalar_prefetch=1,
        grid=(bsz, nj + 1),
        in_specs=[
            pl.BlockSpec((None, SMP_ROWS, HD_B), lambda b, j, pt: (b, 0, 0)),
            pl.BlockSpec((None, PAGE_SIZE, kw), lambda b, j, pt: (b, 0, 0)),
            pl.BlockSpec((None, PAGE_SIZE, kw), lambda b, j, pt: (b, 0, 0)),
            pl.BlockSpec((None, None, 8, n), lambda b, j, pt: (b, j, 0, 0)),
        ] + _page_specs(page, g_count, nj) + _page_specs(page, g_count, nj),
        out_specs=pl.BlockSpec((None, SMP_ROWS, HD_B), lambda b, j, pt: (b, 0, 0)),
        scratch_shapes=[pltpu.VMEM((SMP_ROWS, 1), F32), pltpu.VMEM((SMP_ROWS, 1), F32),
                        pltpu.VMEM((SMP_ROWS, HD_B), F32)],
    )
    return pl.pallas_call(
        functools.partial(_dsa_sample_attn_body, g_count),
        grid_spec=grid_spec,
        out_shape=jax.ShapeDtypeStruct((bsz, SMP_ROWS, HD_B), F32),
        compiler_params=_compiler_params(("parallel", "arbitrary")),
        name="dsa_sample_attn",
    )(page_table, q_rows, k_new, v_new, bias, *([cache_k] * g_count), *([cache_v] * g_count))


def _mla_sample_body(g_count, pt_ref, ql_ref, qr_ref, cnew_ref, rnew_ref, *rest):
    c_refs = rest[:g_count]
    r_refs = rest[g_count:2 * g_count]
    o_ref, m_ref, l_ref, acc_ref = rest[2 * g_count:]
    j = pl.program_id(1)
    nj = pl.num_programs(1) - 1
    contract_last = (((1,), (1,)), ((), ()))
    ql = ql_ref[...].astype(BF16)
    qr = qr_ref[...].astype(BF16)

    @pl.when(j == 0)
    def _():
        m_ref[...] = jnp.full(m_ref.shape, MASKED, F32)
        l_ref[...] = jnp.zeros(l_ref.shape, F32)
        acc_ref[...] = jnp.zeros(acc_ref.shape, F32)

    def update(c, rt, new_tokens):
        s = (lax.dot_general(ql, c, contract_last, preferred_element_type=F32)
             + jnp.dot(qr, rt, preferred_element_type=F32)) * ATT_SCALE_D
        if new_tokens:
            s = jnp.where(_new_token_visible(s.shape), s, MASKED)
        _online_softmax_update(s, c, m_ref, l_ref, acc_ref)

    @pl.when(j < nj)
    def _():
        update(jnp.concatenate([x[...] for x in c_refs], 0).astype(BF16),
               jnp.concatenate([x[...] for x in r_refs], 1).astype(BF16), False)

    @pl.when(j == nj)
    def _():
        update(cnew_ref[...].astype(BF16), rnew_ref[...].astype(BF16), True)
        o_ref[...] = acc_ref[...] / l_ref[...]


def mla_sample_attn(page_table, q_lat, q_rope, c_new, r_new, cache_ckv, cache_krope, g_count=SMP_G):
    bsz, n_pages = page_table.shape
    assert n_pages % g_count == 0
    nj = n_pages // g_count
    grid_spec = pltpu.PrefetchScalarGridSpec(
        num_scalar_prefetch=1,
        grid=(bsz, nj + 1),
        in_specs=[
            pl.BlockSpec((None, SMP_ROWS, R_KV), lambda b, j, pt: (b, 0, 0)),
            pl.BlockSpec((None, SMP_ROWS, DR_D), lambda b, j, pt: (b, 0, 0)),
            pl.BlockSpec((None, PAGE_SIZE, R_KV), lambda b, j, pt: (b, 0, 0)),
            pl.BlockSpec((None, DR_D, PAGE_SIZE), lambda b, j, pt: (b, 0, 0)),
        ] + _page_specs((PAGE_SIZE, R_KV), g_count, nj) + _page_specs((DR_D, PAGE_SIZE), g_count, nj),
        out_specs=pl.BlockSpec((None, SMP_ROWS, R_KV), lambda b, j, pt: (b, 0, 0)),
        scratch_shapes=[pltpu.VMEM((SMP_ROWS, 1), F32), pltpu.VMEM((SMP_ROWS, 1), F32),
                        pltpu.VMEM((SMP_ROWS, R_KV), F32)],
    )
    return pl.pallas_call(
        functools.partial(_mla_sample_body, g_count),
        grid_spec=grid_spec,
        out_shape=jax.ShapeDtypeStruct((bsz, SMP_ROWS, R_KV), F32),
        compiler_params=_compiler_params(("parallel", "arbitrary")),
        name="mla_sample_attn",
    )(page_table, q_lat, q_rope, c_new, r_new, *([cache_ckv] * g_count), *([cache_krope] * g_count))


SCAN_TT = 128
SUBLANE = 8


def _linear_scan_body(a_ref, b_ref, o_ref, h_ref):
    @pl.when(pl.program_id(0) == 0)
    def _():
        h_ref[...] = jnp.zeros_like(h_ref)

    def step(t, h):
        h = a_ref[:, t] * h + b_ref[:, t]
        o_ref[:, t] = h
        return h

    h_ref[...] = lax.fori_loop(0, SCAN_TT, step, h_ref[...], unroll=8)


def linear_scan(a, b):
    bsz, t, c = a.shape
    assert t % SCAN_TT == 0 and c % (SUBLANE * LANE) == 0
    fold = (bsz * (c // (SUBLANE * LANE)), t, SUBLANE, LANE)
    to_tiles = lambda x: jnp.moveaxis(x.reshape(bsz, t, c // (SUBLANE * LANE), SUBLANE, LANE), 2, 1).reshape(fold)
    spec = pl.BlockSpec((fold[0], SCAN_TT, SUBLANE, LANE), lambda i: (0, i, 0, 0))
    h = pl.pallas_call(
        _linear_scan_body,
        grid=(t // SCAN_TT,),
        in_specs=[spec, spec],
        out_specs=spec,
        out_shape=jax.ShapeDtypeStruct(fold, F32),
        scratch_shapes=[pltpu.VMEM((fold[0], SUBLANE, LANE), F32)],
        compiler_params=_compiler_params(("arbitrary",)),
        name="linear_scan",
    )(to_tiles(a), to_tiles(b))
    h = jnp.moveaxis(h.reshape(bsz, c // (SUBLANE * LANE), t, SUBLANE, LANE), 1, 2)
    return h.reshape(bsz, t, c)


PROJ_TM = 256


def _proj_ln_body(x_ref, o_in_ref, w_ref, g_ref, b_ref, o_ref):
    f = jnp.dot(o_in_ref[...].astype(BF16), w_ref[...], preferred_element_type=F32)
    y = DN_ALPHA * x_ref[...] + f
    o_ref[...] = _layer_norm_rows(y, g_ref[...], b_ref[...])


def proj_ln(x, o, w, g, b):
    m, d = x.shape
    k = o.shape[1]
    assert m % PROJ_TM == 0
    return pl.pallas_call(
        _proj_ln_body,
        grid=(m // PROJ_TM,),
        in_specs=[
            pl.BlockSpec((PROJ_TM, d), lambda i: (i, 0)),
            pl.BlockSpec((PROJ_TM, k), lambda i: (i, 0)),
            pl.BlockSpec((k, d), lambda i: (0, 0)),
            pl.BlockSpec((1, d), lambda i: (0, 0)),
            pl.BlockSpec((1, d), lambda i: (0, 0)),
        ],
        out_specs=pl.BlockSpec((PROJ_TM, d), lambda i: (i, 0)),
        out_shape=jax.ShapeDtypeStruct((m, d), F32),
        compiler_params=_compiler_params(("parallel",)),
        name="proj_ln",
    )(x, o, w, g, b)


def _l2_norm(x):
    return x * lax.rsqrt(jnp.sum(x * x, -1, keepdims=True) + RMS_EPS)


def _rms_norm(x, g):
    return x * lax.rsqrt(jnp.mean(x * x, -1, keepdims=True) + RMS_EPS) * g


def _causal_conv(x, buf, w):
    t = x.shape[1]
    xp = jnp.concatenate([buf.astype(x.dtype), x], axis=1)
    y = sum(xp[:, j:j + t] * w[j] for j in range(CONV_W))
    return y, xp[:, t:]


def _rope(x, pos):
    half = DR_D // 2
    inv = ROPE_THETA ** (-jnp.arange(half, dtype=F32) / half)
    ang = pos.astype(F32)[:, None] * inv
    ang = ang.reshape(ang.shape[0], *([1] * (x.ndim - 3)), half)
    cos, sin = jnp.cos(ang), jnp.sin(ang)
    x1, x2 = x[..., :half], x[..., half:]
    return jnp.concatenate([x1 * cos - x2 * sin, x2 * cos + x1 * sin], -1)


def _chunk(a, n, c):
    a = jnp.pad(a, [(0, 0), (0, n * c - a.shape[1])] + [(0, 0)] * (a.ndim - 2))
    a = a.reshape(a.shape[0], n, c, *a.shape[2:])
    return jnp.swapaxes(jnp.swapaxes(a, 0, 1), 2, 3)


def _gated_delta(q, k, v, g, beta, s0):
    bsz, t = q.shape[:2]
    c = min(GDN_CHUNK, t)
    n = -(-t // c)
    qc = _chunk(q * DK_A ** -0.5, n, c)
    kc, vc = _chunk(k, n, c), _chunk(v, n, c)
    gc, bc = _chunk(g, n, c), _chunk(beta, n, c)
    gcum = jnp.cumsum(gc, axis=-1)
    incl = jnp.tril(jnp.ones((c, c), bool))
    strict = jnp.tril(jnp.ones((c, c), bool), -1)
    decay = jnp.exp(jnp.where(incl, gcum[..., :, None] - gcum[..., None, :], -jnp.inf))
    kb = kc * bc[..., None]
    lmat = jnp.where(strict, jnp.einsum('nbhid,nbhjd->nbhij', kb, kc) * decay, 0.0)
    eye = jnp.eye(c, dtype=F32)
    tmat = lax.linalg.triangular_solve(eye + lmat, jnp.broadcast_to(eye, lmat.shape),
                                       left_side=True, lower=True, unit_diagonal=True)
    u = tmat @ (vc * bc[..., None])
    w = tmat @ (kb * jnp.exp(gcum)[..., None])
    qk = jnp.einsum('nbhid,nbhjd->nbhij', qc, kc) * decay
    qg = qc * jnp.exp(gcum)[..., None]
    kd = kc * jnp.exp(gcum[..., -1:] - gcum)[..., None]
    glast = jnp.exp(gcum[..., -1])

    def step(s, xs):
        u_i, w_i, qk_i, qg_i, kd_i, gl_i = xs
        v_new = u_i - jnp.einsum('bhcd,bhde->bhce', w_i, s)
        o = jnp.einsum('bhcd,bhde->bhce', qg_i, s) + jnp.einsum('bhij,bhje->bhie', qk_i, v_new)
        s = s * gl_i[..., None, None] + jnp.einsum('bhcd,bhce->bhde', kd_i, v_new)
        return s, o

    s, o = lax.scan(step, s0, (u, w, qk, qg, kd, glast))
    o = jnp.swapaxes(jnp.swapaxes(o, 2, 3), 0, 1).reshape(bsz, n * c, H_A, DV_A)[:, :t]
    return o, s


def _pad_page(a):
    return jnp.pad(a, ((0, 0), (0, PAGE_SIZE - a.shape[1]), (0, 0)))


def _dsa_sample(q, k, v, qi, ki, wi, cache_k, cache_v, cache_kidx, page_table, g_count=SMP_G):
    bsz, t = q.shape[:2]
    n_grp = H_B // KVH_B
    kw = KVH_B * HD_B
    qi8 = jnp.concatenate([qi, qi], 1).reshape(bsz, 2 * t * HI_B, DI_B)
    wi8 = jnp.concatenate([wi, wi], 1).reshape(bsz, 2 * t * HI_B, 1)
    bias = dsa_sample_select(page_table, qi8, wi8, jnp.swapaxes(_pad_page(ki), 1, 2),
                             jnp.swapaxes(cache_kidx, 1, 2), g_count)
    q_rows = jnp.transpose(q.reshape(bsz, t, KVH_B, n_grp, HD_B), (0, 2, 3, 1, 4)).reshape(bsz, H_B * t, HD_B)
    out = dsa_sample_attn(page_table, q_rows, _pad_page(k.reshape(bsz, t, kw)), _pad_page(v.reshape(bsz, t, kw)), bias,
                          cache_k.reshape(-1, PAGE_SIZE * KVH_B, HD_B), cache_v.reshape(-1, PAGE_SIZE * KVH_B, HD_B),
                          g_count)
    out = out.reshape(bsz, KVH_B, n_grp, t, HD_B)
    return jnp.transpose(out, (0, 3, 1, 2, 4)).reshape(bsz, t, H_B, HD_B)


def _mixer_ab_core(hm, hs, conv_buf, rec_state, sparse_attn, conv_w_a, a_log_a, dt_bias_a, norm_g_a):
    bsz, t, _ = hm.shape
    qkv_pre = hm[..., :A_QKV]
    z = hm[..., A_QKV:A_QKV + 1024]
    q_b = hm[..., 4096:5120]
    qi_b = hm[..., 5120:6144]
    k_b = hm[..., 6144:6400]
    v_b = hm[..., 6400:6656]
    ki_b = hs[..., :DI_B]
    b_gate = hs[..., 64:72]
    a_gate = hs[..., 72:80]
    wi_b = hs[..., 80:96]
    qkv, conv_new = _causal_conv(qkv_pre, conv_buf, conv_w_a)
    qkv = jax.nn.silu(qkv)
    qa = _l2_norm(qkv[..., :1024].reshape(bsz, t, H_A, DK_A))
    ka = _l2_norm(qkv[..., 1024:2048].reshape(bsz, t, H_A, DK_A))
    va = qkv[..., 2048:].reshape(bsz, t, H_A, DV_A)
    beta = jax.nn.sigmoid(b_gate)
    g = -jnp.exp(a_log_a) * jax.nn.softplus(a_gate + dt_bias_a)
    o_a, s_new = _gated_delta(qa, ka, va, g, beta, rec_state)
    o_a = _rms_norm(o_a, norm_g_a) * jax.nn.silu(z.reshape(bsz, t, H_A, DV_A))
    q_b = q_b.reshape(bsz, t, H_B, HD_B)
    k_b = k_b.reshape(bsz, t, KVH_B, HD_B)
    v_b = v_b.reshape(bsz, t, KVH_B, HD_B)
    qi_b = qi_b.reshape(bsz, t, HI_B, DI_B)
    o_b = sparse_attn(q_b, k_b, v_b, qi_b, ki_b, wi_b * IDX_SCALE)
    o = jnp.concatenate([o_a.reshape(bsz, t, -1), o_b.reshape(bsz, t, -1)], -1)
    return o, conv_new, s_new, k_b, v_b, ki_b


def _rglru_branch(xb, yb, conv_buf, h0, conv_w_c, conv_b_c, w_rg_a, b_rg_a, w_rg_x, b_rg_x, lambda_c):
    bsz, t, _ = xb.shape
    xc, conv_new = _causal_conv(xb, conv_buf, conv_w_c)
    xc = xc + conv_b_c
    xr = xc.reshape(bsz, t, NB_C, BW_C)
    r = jax.nn.sigmoid(jnp.einsum('btnd,nde->btne', xr, w_rg_a).reshape(bsz, t, D_C) + b_rg_a)
    i = jax.nn.sigmoid(jnp.einsum('btnd,nde->btne', xr, w_rg_x).reshape(bsz, t, D_C) + b_rg_x)
    log_a = -RG_C * r * jax.nn.softplus(-lambda_c)
    a = jnp.exp(log_a)
    bterm = jnp.sqrt(-jnp.expm1(2.0 * log_a)) * (i * xc)
    bterm = bterm.at[:, 0].add(a[:, 0] * h0)

    if t % SCAN_TT == 0:
        h = linear_scan(a, bterm)
    else:
        def comb(lhs, rhs):
            return lhs[0] * rhs[0], rhs[0] * lhs[1] + rhs[1]

        _, h = lax.associative_scan(comb, (a, bterm), axis=1)
    return h * jax.nn.gelu(yb), conv_new, h[:, -1]


def _mla_prompt(qn, qr, ckv, kr, w_uk, w_uv):
    bsz, s_len = qn.shape[:2]
    m = bsz * s_len
    ckv2 = ckv.reshape(m, R_KV)
    k_nope = matmul(ckv2, w_uk.reshape(R_KV, H_D * DN_D).astype(BF16), H_D * DN_D, BF16)
    v = matmul(ckv2, w_uv.reshape(R_KV, H_D * DV_D).astype(BF16), H_D * DV_D, BF16)
    qr_hm = jnp.moveaxis(qr.reshape(m, H_D, DR_D), 1, 0)
    out = mla_prompt(qn.reshape(m, H_D * DN_D), qr_hm, k_nope, kr.reshape(m, DR_D).astype(BF16), v, bsz, s_len)
    return out.reshape(bsz, s_len, H_D, DV_D)


def _mla_sample(qn, qr, ckv, kr, w_uk, w_uv, cache_ckv, cache_krope, page_table, g_count=SMP_G):
    bsz, t = qn.shape[:2]
    q_lat = jnp.einsum('bthd,rhd->bthr', qn, w_uk)
    rows = lambda a: jnp.swapaxes(a, 1, 2).reshape(bsz, H_D * t, a.shape[-1])
    lat = mla_sample_attn(page_table, rows(q_lat), rows(qr), _pad_page(ckv), jnp.swapaxes(_pad_page(kr), 1, 2),
                          cache_ckv, jnp.swapaxes(cache_krope, 1, 2), g_count)
    lat = jnp.swapaxes(lat.reshape(bsz, H_D, t, R_KV), 1, 2)
    return jnp.einsum('bthr,rhd->bthd', lat, w_uv)


def _mixer_cd_core(hm, hs, conv_buf, h0, pos0, latent_attn, conv_w_c, conv_b_c, w_rg_a, b_rg_a,
                   w_rg_x, b_rg_x, lambda_c, kv_norm_g_d):
    bsz, t, _ = hm.shape
    xb = hm[..., :D_C]
    yb = hm[..., D_C:2 * D_C]
    qn = hm[..., 2048:3072].reshape(bsz, t, H_D, DN_D)
    qr = hm[..., 3072:3584].reshape(bsz, t, H_D, DR_D)
    ckv = hm[..., 3584:3840]
    kr = hs[..., :DR_D]
    o_c, conv_new, h_new = _rglru_branch(xb, yb, conv_buf, h0, conv_w_c, conv_b_c, w_rg_a, b_rg_a,
                                         w_rg_x, b_rg_x, lambda_c)
    pos = pos0 + jnp.arange(t)
    qr = _rope(qr, pos)
    kr = _rope(kr, pos)
    ckv = _rms_norm(ckv, kv_norm_g_d)
    o_d = latent_attn(qn, qr, ckv, kr)
    o = jnp.concatenate([o_c, o_d.reshape(bsz, t, -1)], -1)
    return o, conv_new, h_new, ckv, kr


def _pad_cols(w, n):
    return jnp.pad(w, ((0, 0), (0, n - w.shape[1])))


def kernel(x_prompt, x_sample, state_a_conv, state_a_rec, cache_b_k, cache_b_v, cache_b_kidx, state_c_conv, state_c_h, cache_d_ckv, cache_d_krope, page_table, ln_g, ln_b, ffn_w_gate, ffn_w_up, ffn_w_down, w_in_ab, conv_w_a, a_log_a, dt_bias_a, norm_g_a, w_out_ab, w_in_cd, conv_w_c, conv_b_c, w_rg_a, b_rg_a, w_rg_x, b_rg_x, lambda_c, kv_norm_g_d, w_uk_d, w_uv_d, w_out_cd):
    bp, tp = x_prompt.shape[:2]
    bs, ts = x_sample.shape[:2]
    mp = bp * tp
    past = page_table.shape[1] * PAGE_SIZE
    n_pg = tp // PAGE_SIZE

    x = jnp.concatenate([x_prompt.reshape(mp, D_MODEL), x_sample.reshape(bs * ts, D_MODEL)], 0)

    wg = ffn_w_gate.astype(BF16)
    wu = ffn_w_up.astype(BF16)
    wd = ffn_w_down.astype(BF16)

    ab = np.cumsum((0, A_QKV, H_A * DV_A, H_A, H_A, H_B * HD_B, KVH_B * HD_B, KVH_B * HD_B, HI_B * DI_B, DI_B, HI_B))
    col = lambda w, i: w[:, int(ab[i]):int(ab[i + 1])]
    w_ab_main = jnp.concatenate([col(w_in_ab, 0), col(w_in_ab, 1), col(w_in_ab, 4), col(w_in_ab, 7),
                                 col(w_in_ab, 5), col(w_in_ab, 6)], 1).astype(BF16)
    w_ab_small = _pad_cols(jnp.concatenate([col(w_in_ab, 8), col(w_in_ab, 2), col(w_in_ab, 3), col(w_in_ab, 9)], 1),
                           LANE).astype(BF16)
    w_q_d = w_in_cd[:, 2 * D_C:2 * D_C + H_D * (DN_D + DR_D)].reshape(D_MODEL, H_D, DN_D + DR_D)
    w_cd_main = jnp.concatenate([w_in_cd[:, :2 * D_C],
                                 w_q_d[..., :DN_D].reshape(D_MODEL, H_D * DN_D),
                                 w_q_d[..., DN_D:].reshape(D_MODEL, H_D * DR_D),
                                 w_in_cd[:, 3584:3840]], 1).astype(BF16)
    w_cd_small = _pad_cols(w_in_cd[:, 3840:], LANE).astype(BF16)
    w_out_ab_b = w_out_ab.astype(BF16)
    w_out_cd_b = w_out_cd.astype(BF16)

    def ffn(x, layer, half, ln_idx):
        return ffn_ln(x, wg[layer, half], wu[layer, half], wd[layer, half],
                      ln_g[layer, ln_idx][None], ln_b[layer, ln_idx][None])

    x = ffn(x, 0, 0, 0)
    hm = matmul(x, w_ab_main, 1664)
    hs = matmul(x, w_ab_small, LANE)
    o_p, a_conv_p, a_rec_p, kb_p, vb_p, kib_p = _mixer_ab_core(
        hm[:mp].reshape(bp, tp, -1), hs[:mp].reshape(bp, tp, -1),
        jnp.zeros((bp, CONV_W - 1, A_QKV), F32), jnp.zeros((bp, H_A, DK_A, DV_A), F32),
        lambda *unused: dsa_prompt(hm, hs, bp, tp), conv_w_a, a_log_a, dt_bias_a, norm_g_a)
    attn_s = functools.partial(_dsa_sample, cache_k=cache_b_k, cache_v=cache_b_v,
                               cache_kidx=cache_b_kidx, page_table=page_table)
    o_s, a_conv_s, a_rec_s, b_k_s, b_v_s, b_kidx_s = _mixer_ab_core(
        hm[mp:].reshape(bs, ts, -1), hs[mp:].reshape(bs, ts, -1), state_a_conv, state_a_rec,
        attn_s, conv_w_a, a_log_a, dt_bias_a, norm_g_a)
    b_k_p = kb_p.reshape(bp, n_pg, PAGE_SIZE, KVH_B, HD_B)
    b_v_p = vb_p.reshape(bp, n_pg, PAGE_SIZE, KVH_B, HD_B)
    b_kidx_p = kib_p.reshape(bp, n_pg, PAGE_SIZE, DI_B)
    o = jnp.concatenate([o_p.reshape(mp, -1), o_s.reshape(bs * ts, -1)], 0)
    x = proj_ln(x, o, w_out_ab_b, ln_g[0, 1][None], ln_b[0, 1][None])
    x = ffn(x, 0, 1, 2)

    x = ffn(x, 1, 0, 0)
    hm = matmul(x, w_cd_main, 1920)
    hs = matmul(x, w_cd_small, LANE)
    attn_p = functools.partial(_mla_prompt, w_uk=w_uk_d, w_uv=w_uv_d)
    o_p, c_conv_p, c_h_p, ckv_p, kr_p = _mixer_cd_core(
        hm[:mp].reshape(bp, tp, -1), hs[:mp].reshape(bp, tp, -1),
        jnp.zeros((bp, CONV_W - 1, D_C), F32), jnp.zeros((bp, D_C), F32), 0, attn_p,
        conv_w_c, conv_b_c, w_rg_a, b_rg_a, w_rg_x, b_rg_x, lambda_c, kv_norm_g_d)
    attn_s = functools.partial(_mla_sample, w_uk=w_uk_d, w_uv=w_uv_d, cache_ckv=cache_d_ckv,
                               cache_krope=cache_d_krope, page_table=page_table)
    o_s, c_conv_s, c_h_s, d_ckv_s, d_krope_s = _mixer_cd_core(
        hm[mp:].reshape(bs, ts, -1), hs[mp:].reshape(bs, ts, -1), state_c_conv, state_c_h, past, attn_s,
        conv_w_c, conv_b_c, w_rg_a, b_rg_a, w_rg_x, b_rg_x, lambda_c, kv_norm_g_d)
    d_ckv_p = ckv_p.reshape(bp, n_pg, PAGE_SIZE, R_KV)
    d_krope_p = kr_p.reshape(bp, n_pg, PAGE_SIZE, DR_D)
    o = jnp.concatenate([o_p.reshape(mp, -1), o_s.reshape(bs * ts, -1)], 0)
    x = proj_ln(x, o, w_out_cd_b, ln_g[1, 1][None], ln_b[1, 1][None])
    x = ffn(x, 1, 1, 2)

    yp = x[:mp].reshape(bp, tp, D_MODEL)
    ys = x[mp:].reshape(bs, ts, D_MODEL)
    return (yp, ys,
            a_conv_p, a_conv_s, a_rec_p, a_rec_s,
            b_k_p, b_k_s, b_v_p, b_v_s, b_kidx_p, b_kidx_s,
            c_conv_p, c_conv_s, c_h_p, c_h_s,
            d_ckv_p, d_ckv_s, d_krope_p, d_krope_s)
```

```python
import functools

import jax
import jax.numpy as jnp
import numpy as np
from jax import lax
from jax.experimental import pallas as pl
from jax.experimental.pallas import tpu as pltpu

D_MODEL = 2048
DEPTH = 2
PAGE_SIZE = 128
H_A = 8
DK_A = 128
DV_A = 128
CONV_W = 4
GDN_CHUNK = 64
A_QKV = 2 * H_A * DK_A + H_A * DV_A
H_B = 8
KVH_B = 2
HD_B = 128
HI_B = 16
DI_B = 64
TOPK_MAX = 256
D_C = 1024
NB_C = 8
BW_C = D_C // NB_C
RG_C = 8.0
H_D = 8
DN_D = 128
DR_D = 64
DV_D = 128
R_KV = 256
ROPE_THETA = 10000.0
D_FF = 5632
Q_BLOCK = 128
LN_EPS = 1e-5
RMS_EPS = 1e-6
DN_ALPHA = (2 * DEPTH) ** 0.25
IDX_SCALE = (HI_B * DI_B) ** -0.5
ATT_SCALE_B = HD_B ** -0.5
ATT_SCALE_D = (DN_D + DR_D) ** -0.5
F32 = jnp.float32
BF16 = jnp.bfloat16

VMEM_LIMIT_BYTES = 56 * 1024 * 1024
LANE = 128


def _compiler_params(semantics):
    return pltpu.CompilerParams(dimension_semantics=semantics, vmem_limit_bytes=VMEM_LIMIT_BYTES)


def _layer_norm_rows(y, g, b):
    mu = jnp.mean(y, axis=-1, keepdims=True)
    d = y - mu
    var = jnp.mean(d * d, axis=-1, keepdims=True)
    return d * lax.rsqrt(var + LN_EPS) * g + b


FFN_TM = 768
FFN_TF = 512


def _ffn_ln_body(x_ref, wg_ref, wu_ref, wd_ref, g_ref, b_ref, o_ref, xb_ref, acc_ref):
    j = pl.program_id(1)

    @pl.when(j == 0)
    def _():
        xb_ref[...] = x_ref[...].astype(BF16)
        acc_ref[...] = jnp.zeros_like(acc_ref)

    xb = xb_ref[...]
    hg = jnp.dot(xb, wg_ref[...], preferred_element_type=F32)
    hu = jnp.dot(xb, wu_ref[...], preferred_element_type=F32)
    h = (hg * jax.nn.sigmoid(hg)) * hu
    acc_ref[...] += jnp.dot(h.astype(BF16), wd_ref[...], preferred_element_type=F32)

    @pl.when(j == pl.num_programs(1) - 1)
    def _():
        y = DN_ALPHA * x_ref[...] + 0.5 * acc_ref[...]
        o_ref[...] = _layer_norm_rows(y, g_ref[...], b_ref[...])


def ffn_ln(x, wg, wu, wd, g, b):
    m, d = x.shape
    f = wg.shape[1]
    assert m % FFN_TM == 0 and f % FFN_TF == 0
    return pl.pallas_call(
        _ffn_ln_body,
        grid=(m // FFN_TM, f // FFN_TF),
        in_specs=[
            pl.BlockSpec((FFN_TM, d), lambda i, j: (i, 0)),
            pl.BlockSpec((d, FFN_TF), lambda i, j: (0, j)),
            pl.BlockSpec((d, FFN_TF), lambda i, j: (0, j)),
            pl.BlockSpec((FFN_TF, d), lambda i, j: (j, 0)),
            pl.BlockSpec((1, d), lambda i, j: (0, 0)),
            pl.BlockSpec((1, d), lambda i, j: (0, 0)),
        ],
        out_specs=pl.BlockSpec((FFN_TM, d), lambda i, j: (i, 0)),
        out_shape=jax.ShapeDtypeStruct((m, d), F32),
        scratch_shapes=[pltpu.VMEM((FFN_TM, d), BF16), pltpu.VMEM((FFN_TM, d), F32)],
        compiler_params=_compiler_params(("parallel", "arbitrary")),
        name="ffn_ln",
    )(x, wg, wu, wd, g, b)


MM_TM = 512


def _matmul_body(x_ref, w_ref, o_ref, xb_ref):
    @pl.when(pl.program_id(1) == 0)
    def _():
        xb_ref[...] = x_ref[...].astype(BF16)

    o_ref[...] = jnp.dot(xb_ref[...], w_ref[...], preferred_element_type=F32).astype(o_ref.dtype)


def matmul(x, w, tn, out_dtype=F32):
    m, k = x.shape
    n = w.shape[1]
    assert m % MM_TM == 0 and n % tn == 0
    return pl.pallas_call(
        _matmul_body,
        grid=(m // MM_TM, n // tn),
        in_specs=[
            pl.BlockSpec((MM_TM, k), lambda i, j: (i, 0)),
            pl.BlockSpec((k, tn), lambda i, j: (0, j)),
        ],
        out_specs=pl.BlockSpec((MM_TM, tn), lambda i, j: (i, j)),
        out_shape=jax.ShapeDtypeStruct((m, n), out_dtype),
        scratch_shapes=[pltpu.VMEM((MM_TM, k), BF16)],
        compiler_params=_compiler_params(("parallel", "arbitrary")),
        name="matmul",
    )(x, w)


DSA_TQ = 128
DSA_KC = 1024
DSA_COUNT_ROWS = 512
MASKED = -1e30
INT32_MIN = -2 ** 31
AB_Q_COL, AB_QI_COL, AB_K_COL, AB_V_COL = 4096, 5120, 6144, 6400
AB_WI_LANE = 80


def _order_key(x):
    bits = lax.bitcast_convert_type(jnp.where(x == 0.0, 0.0, x), jnp.int32)
    return jnp.where(bits < 0, bits ^ 0x7FFFFFFF, bits)


def _dsa_prompt_body(ksel, q_ref, qi_ref, hsq_ref, k_ref, v_ref, hsk_ref, o_ref,
                     kb_ref, vb_ref, kk_ref, keyt_ref, bias_ref, m_ref, l_ref, acc_ref):
    i = pl.program_id(1)
    tq, kc = DSA_TQ, DSA_KC
    n_all = kk_ref.shape[0] // kc
    n_grp = H_B // KVH_B
    contract_last = (((1,), (1,)), ((), ()))

    @pl.when(i == 0)
    def _():
        def cast_chunk(c, carry):
            rows = pl.ds(pl.multiple_of(c * kc, kc), kc)
            kf = k_ref[rows, :]
            vf = v_ref[rows, :]
            for g in range(KVH_B):
                kb_ref[g, rows, :] = kf[:, g * HD_B:(g + 1) * HD_B].astype(BF16)
                vb_ref[g, rows, :] = vf[:, g * HD_B:(g + 1) * HD_B].astype(BF16)
            hs = hsk_ref[rows, :]
            lane = lax.broadcasted_iota(jnp.int32, hs.shape, 1)
            kk_ref[rows, :] = jnp.where(lane < DI_B, hs, pltpu.roll(hs, DI_B, 1)).astype(BF16)
            return carry

        lax.fori_loop(0, n_all, cast_chunk, 0)

    nk = (i * tq + tq + kc - 1) // kc

    qi = qi_ref[...]
    lane = lax.broadcasted_iota(jnp.int32, (tq, 2 * DI_B), 1)
    rhs = []
    for j in range(HI_B // 2):
        pair = qi[:, j * 2 * DI_B:(j + 1) * 2 * DI_B]
        rhs.append(jnp.concatenate([jnp.where(lane < DI_B, pair, 0.0),
                                    jnp.where(lane >= DI_B, pair, 0.0)], 0).astype(BF16))
    wi_t = jnp.transpose(hsq_ref[...])[AB_WI_LANE:AB_WI_LANE + HI_B, :] * IDX_SCALE
    t_pos = i * tq + lax.broadcasted_iota(jnp.int32, (kc, tq), 1)

    def score_chunk(c, carry):
        kk = kk_ref[pl.ds(pl.multiple_of(c * kc, kc), kc), :]
        sc = jnp.zeros((kc, tq), F32)
        for j in range(HI_B // 2):
            o = lax.dot_general(kk, rhs[j], contract_last, preferred_element_type=F32)
            sc = sc + wi_t[2 * j:2 * j + 1, :] * jnp.maximum(o[:, :tq], 0.0)
            sc = sc + wi_t[2 * j + 1:2 * j + 2, :] * jnp.maximum(o[:, tq:], 0.0)
        s_pos = c * kc + lax.broadcasted_iota(jnp.int32, (kc, tq), 0)
        keyt_ref[c] = jnp.where(s_pos <= t_pos, _order_key(sc), INT32_MIN)
        return carry

    lax.fori_loop(0, nk, score_chunk, 0)

    def count_ge(trial):
        def body(c, cnt):
            for r0 in range(0, kc, DSA_COUNT_ROWS):
                ge = (keyt_ref[c, r0:r0 + DSA_COUNT_ROWS, :] >= trial).astype(jnp.int32)
                cnt = cnt + jnp.sum(ge.reshape(DSA_COUNT_ROWS // 8, 8, tq), axis=0)
            return cnt

        cnt = lax.fori_loop(0, nk, body, jnp.zeros((8, tq), jnp.int32))
        return jnp.sum(cnt, axis=0, keepdims=True)

    def bit_pass(it, prefix):
        trial = prefix | lax.shift_left(jnp.int32(1), 31 - it)
        return jnp.where(count_ge(trial ^ INT32_MIN) >= ksel, trial, prefix)

    prefix = lax.fori_loop(0, 32, bit_pass, jnp.zeros((1, tq), jnp.int32))
    thr = jnp.maximum(prefix ^ INT32_MIN, INT32_MIN + 1)

    def bias_chunk(c, carry):
        bias_ref[c] = jnp.transpose(jnp.where(keyt_ref[c] >= thr, 0.0, MASKED))
        return carry

    lax.fori_loop(0, nk, bias_chunk, 0)

    q = q_ref[...]
    qg = [jnp.concatenate([q[:, (g * n_grp + hh) * HD_B:(g * n_grp + hh + 1) * HD_B] for hh in range(n_grp)],
                          0).astype(BF16) for g in range(KVH_B)]
    m_ref[...] = jnp.full(m_ref.shape, MASKED, F32)
    l_ref[...] = jnp.zeros(l_ref.shape, F32)
    acc_ref[...] = jnp.zeros(acc_ref.shape, F32)

    def attn_chunk(c, carry):
        rows = pl.ds(pl.multiple_of(c * kc, kc), kc)
        bias = bias_ref[c]
        for g in range(KVH_B):
            s = lax.dot_general(qg[g], kb_ref[g, rows, :], contract_last, preferred_element_type=F32) * ATT_SCALE_B
            s = (s.reshape(n_grp, tq, kc) + bias[None]).reshape(n_grp * tq, kc)
            m_old = m_ref[g]
            m_new = jnp.maximum(m_old, jnp.max(s, axis=1, keepdims=True))
            p = jnp.exp(s - m_new)
            corr = jnp.exp(m_old - m_new)
            l_ref[g] = l_ref[g] * corr + jnp.sum(p, axis=1, keepdims=True)
            acc_ref[g] = acc_ref[g] * corr + jnp.dot(p.astype(BF16), vb_ref[g, rows, :], preferred_element_type=F32)
            m_ref[g] = m_new
        return carry

    lax.fori_loop(0, nk, attn_chunk, 0)

    for g in range(KVH_B):
        o = acc_ref[g] / l_ref[g]
        for hh in range(n_grp):
            o_ref[:, (g * n_grp + hh) * HD_B:(g * n_grp + hh + 1) * HD_B] = o[hh * tq:(hh + 1) * tq, :]


def dsa_prompt(hm, hs, bsz, t):
    tq, kc = DSA_TQ, DSA_KC
    assert t % kc == 0 and kc % tq == 0
    nq = t // tq
    ksel = min(TOPK_MAX, t // 4)
    qw = H_B * HD_B
    kw = KVH_B * HD_B
    n_grp = H_B // KVH_B
    return pl.pallas_call(
        functools.partial(_dsa_prompt_body, ksel),
        grid=(bsz, nq),
        in_specs=[
            pl.BlockSpec((tq, qw), lambda b, i: (b * nq + i, AB_Q_COL // qw)),
            pl.BlockSpec((tq, HI_B * DI_B), lambda b, i: (b * nq + i, AB_QI_COL // (HI_B * DI_B))),
            pl.BlockSpec((tq, LANE), lambda b, i: (b * nq + i, 0)),
            pl.BlockSpec((t, kw), lambda b, i: (b, AB_K_COL // kw)),
            pl.BlockSpec((t, kw), lambda b, i: (b, AB_V_COL // kw)),
            pl.BlockSpec((t, LANE), lambda b, i: (b, 0)),
        ],
        out_specs=pl.BlockSpec((tq, qw), lambda b, i: (b * nq + i, 0)),
        out_shape=jax.ShapeDtypeStruct((bsz * t, qw), F32),
        scratch_shapes=[
            pltpu.VMEM((KVH_B, t, HD_B), BF16),
            pltpu.VMEM((KVH_B, t, HD_B), BF16),
            pltpu.VMEM((t, LANE), BF16),
            pltpu.VMEM((t // kc, kc, tq), jnp.int32),
            pltpu.VMEM((t // kc, tq, kc), F32),
            pltpu.VMEM((KVH_B, n_grp * tq, 1), F32),
            pltpu.VMEM((KVH_B, n_grp * tq, 1), F32),
            pltpu.VMEM((KVH_B, n_grp * tq, HD_B), F32),
        ],
        compiler_params=_compiler_params(("parallel", "arbitrary")),
        name="dsa_prompt",
    )(hm, hm, hs, hm, hm, hs)


MLA_TQ = 512


def _mla_prompt_body(qn_ref, qr_ref, kn_ref, kr_ref, v_ref, o_ref, kcat_ref, m_ref, l_ref, acc_ref):
    i = pl.program_id(2)
    tq = MLA_TQ
    contract_last = (((1,), (1,)), ((), ()))
    pad = kcat_ref.shape[1] - DN_D - DR_D

    @pl.when(i == 0)
    def _():
        def build(c, carry):
            rows = pl.ds(pl.multiple_of(c * tq, tq), tq)
            kcat_ref[rows, :] = jnp.concatenate([kn_ref[rows, :], kr_ref[rows, :], jnp.zeros((tq, pad), BF16)], 1)
            return carry

        lax.fori_loop(0, kcat_ref.shape[0] // tq, build, 0)

    q = jnp.concatenate([qn_ref[...], qr_ref[...], jnp.zeros((tq, pad), F32)], 1).astype(BF16)
    m_ref[...] = jnp.full(m_ref.shape, MASKED, F32)
    l_ref[...] = jnp.zeros(l_ref.shape, F32)
    acc_ref[...] = jnp.zeros(acc_ref.shape, F32)

    def chunk(c, diagonal):
        rows = pl.ds(pl.multiple_of(c * tq, tq), tq)
        s = lax.dot_general(q, kcat_ref[rows, :], contract_last, preferred_element_type=F32) * ATT_SCALE_D
        if diagonal:
            causal = (lax.broadcasted_iota(jnp.int32, (tq, tq), 1) <= lax.broadcasted_iota(jnp.int32, (tq, tq), 0))
            s = jnp.where(causal, s, MASKED)
        m_old = m_ref[...]
        m_new = jnp.maximum(m_old, jnp.max(s, axis=1, keepdims=True))
        p = jnp.exp(s - m_new)
        corr = jnp.exp(m_old - m_new)
        l_ref[...] = l_ref[...] * corr + jnp.sum(p, axis=1, keepdims=True)
        acc_ref[...] = acc_ref[...] * corr + jnp.dot(p.astype(BF16), v_ref[rows, :], preferred_element_type=F32)
        m_ref[...] = m_new

    def below_diagonal(c, carry):
        chunk(c, False)
        return carry

    lax.fori_loop(0, i, below_diagonal, 0)
    chunk(i, True)
    o_ref[...] = acc_ref[...] / l_ref[...]


def mla_prompt(qn, qr, kn, kr, v, bsz, t):
    tq = MLA_TQ
    assert t % tq == 0
    nq = t // tq
    return pl.pallas_call(
        _mla_prompt_body,
        grid=(bsz, H_D, nq),
        in_specs=[
            pl.BlockSpec((tq, DN_D), lambda b, h, i: (b * nq + i, h)),
            pl.BlockSpec((None, tq, DR_D), lambda b, h, i: (h, b * nq + i, 0)),
            pl.BlockSpec((t, DN_D), lambda b, h, i: (b, h)),
            pl.BlockSpec((t, DR_D), lambda b, h, i: (b, 0)),
            pl.BlockSpec((t, DV_D), lambda b, h, i: (b, h)),
        ],
        out_specs=pl.BlockSpec((tq, DV_D), lambda b, h, i: (b * nq + i, h)),
        out_shape=jax.ShapeDtypeStruct((bsz * t, H_D * DV_D), F32),
        scratch_shapes=[
            pltpu.VMEM((t, 2 * LANE), BF16),
            pltpu.VMEM((tq, 1), F32),
            pltpu.VMEM((tq, 1), F32),
            pltpu.VMEM((tq, DV_D), F32),
        ],
        compiler_params=_compiler_params(("parallel", "parallel", "arbitrary")),
        name="mla_prompt",
    )(qn, qr, kn, kr, v)


SMP_G = 16
SMP_ROWS = 32
SMP_T = 4


def _page_specs(page_shape, g_count, nj):
    zeros = (0,) * len(page_shape)

    def make(g):
        return pl.BlockSpec((None,) + page_shape,
                            lambda b, j, pt: (pt[b, jnp.minimum(j, nj - 1) * g_count + g],) + zeros)

    return [make(g) for g in range(g_count)]


def _new_token_visible(shape):
    token = lax.broadcasted_iota(jnp.int32, shape, 0) & (SMP_T - 1)
    return lax.broadcasted_iota(jnp.int32, shape, 1) <= token


def _online_softmax_update(s, v, m_ref, l_ref, acc_ref):
    m_old = m_ref[...]
    m_new = jnp.maximum(m_old, jnp.max(s, axis=1, keepdims=True))
    p = jnp.exp(s - m_new)
    corr = jnp.exp(m_old - m_new)
    l_ref[...] = l_ref[...] * corr + jnp.sum(p, axis=1, keepdims=True)
    acc_ref[...] = acc_ref[...] * corr + jnp.dot(p.astype(BF16), v, preferred_element_type=F32)
    m_ref[...] = m_new


def _dsa_sample_select_body(ksel, g_count, pt_ref, qi_ref, wi_ref, knew_ref, *rest):
    k_refs = rest[:g_count]
    bias_ref, key_ref = rest[g_count:]
    j = pl.program_id(1)
    nj = pl.num_programs(1) - 1
    n = g_count * PAGE_SIZE
    qi = qi_ref[...].astype(BF16)
    wi = wi_ref[...]

    def keys_of(kt):
        o = jnp.dot(qi, kt, preferred_element_type=F32)
        r = jnp.maximum(o, 0.0) * wi
        return _order_key(jnp.sum(r.reshape(8, HI_B, kt.shape[1]), axis=1))

    @pl.when(j < nj)
    def _():
        key_ref[j] = keys_of(jnp.concatenate([r[...] for r in k_refs], 1).astype(BF16))

    @pl.when(j == nj)
    def _():
        new = jnp.where(_new_token_visible((8, PAGE_SIZE)), keys_of(knew_ref[...].astype(BF16)), INT32_MIN)
        key_ref[nj] = jnp.concatenate([new, jnp.full((8, n - PAGE_SIZE), INT32_MIN, jnp.int32)], 1)

        def count_ge(trial):
            def body(c, cnt):
                return cnt + (key_ref[c] >= trial).astype(jnp.int32)

            cnt = lax.fori_loop(0, key_ref.shape[0], body, jnp.zeros((8, n), jnp.int32), unroll=True)
            return jnp.sum(cnt, axis=1, keepdims=True)

        def bit_pass(it, prefix):
            trial = prefix | lax.shift_left(jnp.int32(1), 31 - it)
            return jnp.where(count_ge(trial ^ INT32_MIN) >= ksel, trial, prefix)

        prefix = lax.fori_loop(0, 32, bit_pass, jnp.zeros((8, 1), jnp.int32))
        thr = jnp.maximum(prefix ^ INT32_MIN, INT32_MIN + 1)

        def write(c, carry):
            bias_ref[c] = jnp.where(key_ref[c] >= thr, 0.0, MASKED)
            return carry

        lax.fori_loop(0, nj + 1, write, 0)


def dsa_sample_select(page_table, qi8, wi8, ki_new, cache_kidx, g_count=SMP_G):
    bsz, n_pages = page_table.shape
    assert n_pages % g_count == 0
    nj = n_pages // g_count
    n = g_count * PAGE_SIZE
    ksel = min(TOPK_MAX, (n_pages * PAGE_SIZE + SMP_T) // 4)
    grid_spec = pltpu.PrefetchScalarGridSpec(
        num_scalar_prefetch=1,
        grid=(bsz, nj + 1),
        in_specs=[
            pl.BlockSpec((None, 8 * HI_B, DI_B), lambda b, j, pt: (b, 0, 0)),
            pl.BlockSpec((None, 8 * HI_B, 1), lambda b, j, pt: (b, 0, 0)),
            pl.BlockSpec((None, DI_B, PAGE_SIZE), lambda b, j, pt: (b, 0, 0)),
        ] + _page_specs((DI_B, PAGE_SIZE), g_count, nj),
        out_specs=pl.BlockSpec((None, nj + 1, 8, n), lambda b, j, pt: (b, 0, 0, 0)),
        scratch_shapes=[pltpu.VMEM((nj + 1, 8, n), jnp.int32)],
    )
    return pl.pallas_call(
        functools.partial(_dsa_sample_select_body, ksel, g_count),
        grid_spec=grid_spec,
        out_shape=jax.ShapeDtypeStruct((bsz, nj + 1, 8, n), F32),
        compiler_params=_compiler_params(("parallel", "arbitrary")),
        name="dsa_sample_select",
    )(page_table, qi8, wi8, ki_new, *([cache_kidx] * g_count))


def _dsa_sample_attn_body(g_count, pt_ref, q_ref, knew_ref, vnew_ref, bias_ref, *rest):
    k_refs = rest[:g_count]
    v_refs = rest[g_count:2 * g_count]
    o_ref, m_ref, l_ref, acc_ref = rest[2 * g_count:]
    j = pl.program_id(1)
    nj = pl.num_programs(1) - 1
    contract_last = (((1,), (1,)), ((), ()))
    rows_g = SMP_ROWS // KVH_B
    q = q_ref[...].astype(BF16)

    @pl.when(j == 0)
    def _():
        m_ref[...] = jnp.full(m_ref.shape, MASKED, F32)
        l_ref[...] = jnp.zeros(l_ref.shape, F32)
        acc_ref[...] = jnp.zeros(acc_ref.shape, F32)

    def update(ks, vs, bias8):
        s = jnp.concatenate([lax.dot_general(q[g * rows_g:(g + 1) * rows_g], ks[g], contract_last,
                                             preferred_element_type=F32) for g in range(KVH_B)], 0) * ATT_SCALE_B
        s = s + jnp.concatenate([bias8] * (SMP_ROWS // 8), 0)
        m_old = m_ref[...]
        m_new = jnp.maximum(m_old, jnp.max(s, axis=1, keepdims=True))
        p = jnp.exp(s - m_new)
        corr = jnp.exp(m_old - m_new)
        l_ref[...] = l_ref[...] * corr + jnp.sum(p, axis=1, keepdims=True)
        pb = p.astype(BF16)
        pv = jnp.concatenate([jnp.dot(pb[g * rows_g:(g + 1) * rows_g], vs[g], preferred_element_type=F32)
                              for g in range(KVH_B)], 0)
        acc_ref[...] = acc_ref[...] * corr + pv
        m_ref[...] = m_new

    @pl.when(j < nj)
    def _():
        head_rows = lambda r, g: r[pl.ds(g, PAGE_SIZE, stride=KVH_B), :]
        ks = [jnp.concatenate([head_rows(r, g) for r in k_refs], 0).astype(BF16) for g in range(KVH_B)]
        vs = [jnp.concatenate([head_rows(r, g) for r in v_refs], 0).astype(BF16) for g in range(KVH_B)]
        update(ks, vs, bias_ref[...])

    @pl.when(j == nj)
    def _():
        kn = knew_ref[...].astype(BF16)
        vn = vnew_ref[...].astype(BF16)
        update([kn[:, g * HD_B:(g + 1) * HD_B] for g in range(KVH_B)],
               [vn[:, g * HD_B:(g + 1) * HD_B] for g in range(KVH_B)], bias_ref[:, :PAGE_SIZE])
        o_ref[...] = acc_ref[...] / l_ref[...]


def dsa_sample_attn(page_table, q_rows, k_new, v_new, bias, cache_k, cache_v, g_count=SMP_G):
    bsz, n_pages = page_table.shape
    nj = n_pages // g_count
    n = g_count * PAGE_SIZE
    kw = KVH_B * HD_B
    page = (PAGE_SIZE * KVH_B, HD_B)
    grid_spec = pltpu.PrefetchScalarGridSpec(
        num_scalar_prefetch=1,
        grid=(bsz, nj + 1),
        in_specs=[
            pl.BlockSpec((None, SMP_ROWS, HD_B), lambda b, j, pt: (b, 0, 0)),
            pl.BlockSpec((None, PAGE_SIZE, kw), lambda b, j, pt: (b, 0, 0)),
            pl.BlockSpec((None, PAGE_SIZE, kw), lambda b, j, pt: (b, 0, 0)),
            pl.BlockSpec((None, None, 8, n), lambda b, j, pt: (b, j, 0, 0)),
        ] + _page_specs(page, g_count, nj) + _page_specs(page, g_count, nj),
        out_specs=pl.BlockSpec((None, SMP_ROWS, HD_B), lambda b, j, pt: (b, 0, 0)),
        scratch_shapes=[pltpu.VMEM((SMP_ROWS, 1), F32), pltpu.VMEM((SMP_ROWS, 1), F32),
                        pltpu.VMEM((SMP_ROWS, HD_B), F32)],
    )
    return pl.pallas_call(
        functools.partial(_dsa_sample_attn_body, g_count),
        grid_spec=grid_spec,
        out_shape=jax.ShapeDtypeStruct((bsz, SMP_ROWS, HD_B), F32),
        compiler_params=_compiler_params(("parallel", "arbitrary")),
        name="dsa_sample_attn",
    )(page_table, q_rows, k_new, v_new, bias, *([cache_k] * g_count), *([cache_v] * g_count))


def _mla_sample_body(g_count, pt_ref, ql_ref, qr_ref, cnew_ref, rnew_ref, *rest):
    c_refs = rest[:g_count]
    r_refs = rest[g_count:2 * g_count]
    o_ref, m_ref, l_ref, acc_ref = rest[2 * g_count:]
    j = pl.program_id(1)
    nj = pl.num_programs(1) - 1
    contract_last = (((1,), (1,)), ((), ()))
    ql = ql_ref[...].astype(BF16)
    qr = qr_ref[...].astype(BF16)

    @pl.when(j == 0)
    def _():
        m_ref[...] = jnp.full(m_ref.shape, MASKED, F32)
        l_ref[...] = jnp.zeros(l_ref.shape, F32)
        acc_ref[...] = jnp.zeros(acc_ref.shape, F32)

    def update(c, rt, new_tokens):
        s = (lax.dot_general(ql, c, contract_last, preferred_element_type=F32)
             + jnp.dot(qr, rt, preferred_element_type=F32)) * ATT_SCALE_D
        if new_tokens:
            s = jnp.where(_new_token_visible(s.shape), s, MASKED)
        _online_softmax_update(s, c, m_ref, l_ref, acc_ref)

    @pl.when(j < nj)
    def _():
        update(jnp.concatenate([x[...] for x in c_refs], 0).astype(BF16),
               jnp.concatenate([x[...] for x in r_refs], 1).astype(BF16), False)

    @pl.when(j == nj)
    def _():
        update(cnew_ref[...].astype(BF16), rnew_ref[...].astype(BF16), True)
        o_ref[...] = acc_ref[...] / l_ref[...]


def mla_sample_attn(page_table, q_lat, q_rope, c_new, r_new, cache_ckv, cache_krope, g_count=SMP_G):
    bsz, n_pages = page_table.shape
    assert n_pages % g_count == 0
    nj = n_pages // g_count
    grid_spec = pltpu.PrefetchScalarGridSpec(
        num_scalar_prefetch=1,
        grid=(bsz, nj + 1),
        in_specs=[
            pl.BlockSpec((None, SMP_ROWS, R_KV), lambda b, j, pt: (b, 0, 0)),
            pl.BlockSpec((None, SMP_ROWS, DR_D), lambda b, j, pt: (b, 0, 0)),
            pl.BlockSpec((None, PAGE_SIZE, R_KV), lambda b, j, pt: (b, 0, 0)),
            pl.BlockSpec((None, DR_D, PAGE_SIZE), lambda b, j, pt: (b, 0, 0)),
        ] + _page_specs((PAGE_SIZE, R_KV), g_count, nj) + _page_specs((DR_D, PAGE_SIZE), g_count, nj),
        out_specs=pl.BlockSpec((None, SMP_ROWS, R_KV), lambda b, j, pt: (b, 0, 0)),
        scratch_shapes=[pltpu.VMEM((SMP_ROWS, 1), F32), pltpu.VMEM((SMP_ROWS, 1), F32),
                        pltpu.VMEM((SMP_ROWS, R_KV), F32)],
    )
    return pl.pallas_call(
        functools.partial(_mla_sample_body, g_count),
        grid_spec=grid_spec,
        out_shape=jax.ShapeDtypeStruct((bsz, SMP_ROWS, R_KV), F32),
        compiler_params=_compiler_params(("parallel", "arbitrary")),
        name="mla_sample_attn",
    )(page_table, q_lat, q_rope, c_new, r_new, *([cache_ckv] * g_count), *([cache_krope] * g_count))


SCAN_TT = 128
SUBLANE = 8


def _linear_scan_body(a_ref, b_ref, o_ref, h_ref):
    @pl.when(pl.program_id(0) == 0)
    def _():
        h_ref[...] = jnp.zeros_like(h_ref)

    def step(t, h):
        h = a_ref[:, t] * h + b_ref[:, t]
        o_ref[:, t] = h
        return h

    h_ref[...] = lax.fori_loop(0, SCAN_TT, step, h_ref[...], unroll=8)


def linear_scan(a, b):
    bsz, t, c = a.shape
    assert t % SCAN_TT == 0 and c % (SUBLANE * LANE) == 0
    fold = (bsz * (c // (SUBLANE * LANE)), t, SUBLANE, LANE)
    to_tiles = lambda x: jnp.moveaxis(x.reshape(bsz, t, c // (SUBLANE * LANE), SUBLANE, LANE), 2, 1).reshape(fold)
    spec = pl.BlockSpec((fold[0], SCAN_TT, SUBLANE, LANE), lambda i: (0, i, 0, 0))
    h = pl.pallas_call(
        _linear_scan_body,
        grid=(t // SCAN_TT,),
        in_specs=[spec, spec],
        out_specs=spec,
        out_shape=jax.ShapeDtypeStruct(fold, F32),
        scratch_shapes=[pltpu.VMEM((fold[0], SUBLANE, LANE), F32)],
        compiler_params=_compiler_params(("arbitrary",)),
        name="linear_scan",
    )(to_tiles(a), to_tiles(b))
    h = jnp.moveaxis(h.reshape(bsz, c // (SUBLANE * LANE), t, SUBLANE, LANE), 1, 2)
    return h.reshape(bsz, t, c)


PROJ_TM = 256


def _proj_ln_body(x_ref, o_in_ref, w_ref, g_ref, b_ref, o_ref):
    f = jnp.dot(o_in_ref[...].astype(BF16), w_ref[...], preferred_element_type=F32)
    y = DN_ALPHA * x_ref[...] + f
    o_ref[...] = _layer_norm_rows(y, g_ref[...], b_ref[...])


def proj_ln(x, o, w, g, b):
    m, d = x.shape
    k = o.shape[1]
    assert m % PROJ_TM == 0
    return pl.pallas_call(
        _proj_ln_body,
        grid=(m // PROJ_TM,),
        in_specs=[
            pl.BlockSpec((PROJ_TM, d), lambda i: (i, 0)),
            pl.BlockSpec((PROJ_TM, k), lambda i: (i, 0)),
            pl.BlockSpec((k, d), lambda i: (0, 0)),
            pl.BlockSpec((1, d), lambda i: (0, 0)),
            pl.BlockSpec((1, d), lambda i: (0, 0)),
        ],
        out_specs=pl.BlockSpec((PROJ_TM, d), lambda i: (i, 0)),
        out_shape=jax.ShapeDtypeStruct((m, d), F32),
        compiler_params=_compiler_params(("parallel",)),
        name="proj_ln",
    )(x, o, w, g, b)


def _l2_norm(x):
    return x * lax.rsqrt(jnp.sum(x * x, -1, keepdims=True) + RMS_EPS)


def _rms_norm(x, g):
    return x * lax.rsqrt(jnp.mean(x * x, -1, keepdims=True) + RMS_EPS) * g


def _causal_conv(x, buf, w):
    t = x.shape[1]
    xp = jnp.concatenate([buf.astype(x.dtype), x], axis=1)
    y = sum(xp[:, j:j + t] * w[j] for j in range(CONV_W))
    return y, xp[:, t:]


def _rope(x, pos):
    half = DR_D // 2
    inv = ROPE_THETA ** (-jnp.arange(half, dtype=F32) / half)
    ang = pos.astype(F32)[:, None] * inv
    ang = ang.reshape(ang.shape[0], *([1] * (x.ndim - 3)), half)
    cos, sin = jnp.cos(ang), jnp.sin(ang)
    x1, x2 = x[..., :half], x[..., half:]
    return jnp.concatenate([x1 * cos - x2 * sin, x2 * cos + x1 * sin], -1)


def _chunk(a, n, c):
    a = jnp.pad(a, [(0, 0), (0, n * c - a.shape[1])] + [(0, 0)] * (a.ndim - 2))
    a = a.reshape(a.shape[0], n, c, *a.shape[2:])
    return jnp.swapaxes(jnp.swapaxes(a, 0, 1), 2, 3)


def _gated_delta(q, k, v, g, beta, s0):
    bsz, t = q.shape[:2]
    c = min(GDN_CHUNK, t)
    n = -(-t // c)
    qc = _chunk(q * DK_A ** -0.5, n, c)
    kc, vc = _chunk(k, n, c), _chunk(v, n, c)
    gc, bc = _chunk(g, n, c), _chunk(beta, n, c)
    gcum = jnp.cumsum(gc, axis=-1)
    incl = jnp.tril(jnp.ones((c, c), bool))
    strict = jnp.tril(jnp.ones((c, c), bool), -1)
    decay = jnp.exp(jnp.where(incl, gcum[..., :, None] - gcum[..., None, :], -jnp.inf))
    kb = kc * bc[..., None]
    lmat = jnp.where(strict, jnp.einsum('nbhid,nbhjd->nbhij', kb, kc) * decay, 0.0)
    eye = jnp.eye(c, dtype=F32)
    tmat = lax.linalg.triangular_solve(eye + lmat, jnp.broadcast_to(eye, lmat.shape),
                                       left_side=True, lower=True, unit_diagonal=True)
    u = tmat @ (vc * bc[..., None])
    w = tmat @ (kb * jnp.exp(gcum)[..., None])
    qk = jnp.einsum('nbhid,nbhjd->nbhij', qc, kc) * decay
    qg = qc * jnp.exp(gcum)[..., None]
    kd = kc * jnp.exp(gcum[..., -1:] - gcum)[..., None]
    glast = jnp.exp(gcum[..., -1])

    def step(s, xs):
        u_i, w_i, qk_i, qg_i, kd_i, gl_i = xs
        v_new = u_i - jnp.einsum('bhcd,bhde->bhce', w_i, s)
        o = jnp.einsum('bhcd,bhde->bhce', qg_i, s) + jnp.einsum('bhij,bhje->bhie', qk_i, v_new)
        s = s * gl_i[..., None, None] + jnp.einsum('bhcd,bhce->bhde', kd_i, v_new)
        return s, o

    s, o = lax.scan(step, s0, (u, w, qk, qg, kd, glast))
    o = jnp.swapaxes(jnp.swapaxes(o, 2, 3), 0, 1).reshape(bsz, n * c, H_A, DV_A)[:, :t]
    return o, s


def _pad_page(a):
    return jnp.pad(a, ((0, 0), (0, PAGE_SIZE - a.shape[1]), (0, 0)))


def _dsa_sample(q, k, v, qi, ki, wi, cache_k, cache_v, cache_kidx, page_table, g_count=SMP_G):
    bsz, t = q.shape[:2]
    n_grp = H_B // KVH_B
    kw = KVH_B * HD_B
    qi8 = jnp.concatenate([qi, qi], 1).reshape(bsz, 2 * t * HI_B, DI_B)
    wi8 = jnp.concatenate([wi, wi], 1).reshape(bsz, 2 * t * HI_B, 1)
    bias = dsa_sample_select(page_table, qi8, wi8, jnp.swapaxes(_pad_page(ki), 1, 2),
                             jnp.swapaxes(cache_kidx, 1, 2), g_count)
    q_rows = jnp.transpose(q.reshape(bsz, t, KVH_B, n_grp, HD_B), (0, 2, 3, 1, 4)).reshape(bsz, H_B * t, HD_B)
    out = dsa_sample_attn(page_table, q_rows, _pad_page(k.reshape(bsz, t, kw)), _pad_page(v.reshape(bsz, t, kw)), bias,
                          cache_k.reshape(-1, PAGE_SIZE * KVH_B, HD_B), cache_v.reshape(-1, PAGE_SIZE * KVH_B, HD_B),
                          g_count)
    out = out.reshape(bsz, KVH_B, n_grp, t, HD_B)
    return jnp.transpose(out, (0, 3, 1, 2, 4)).reshape(bsz, t, H_B, HD_B)


def _mixer_ab_core(hm, hs, conv_buf, rec_state, sparse_attn, conv_w_a, a_log_a, dt_bias_a, norm_g_a):
    bsz, t, _ = hm.shape
    qkv_pre = hm[..., :A_QKV]
    z = hm[..., A_QKV:A_QKV + 1024]
    q_b = hm[..., 4096:5120]
    qi_b = hm[..., 5120:6144]
    k_b = hm[..., 6144:6400]
    v_b = hm[..., 6400:6656]
    ki_b = hs[..., :DI_B]
    b_gate = hs[..., 64:72]
    a_gate = hs[..., 72:80]
    wi_b = hs[..., 80:96]
    qkv, conv_new = _causal_conv(qkv_pre, conv_buf, conv_w_a)
    qkv = jax.nn.silu(qkv)
    qa = _l2_norm(qkv[..., :1024].reshape(bsz, t, H_A, DK_A))
    ka = _l2_norm(qkv[..., 1024:2048].reshape(bsz, t, H_A, DK_A))
    va = qkv[..., 2048:].reshape(bsz, t, H_A, DV_A)
    beta = jax.nn.sigmoid(b_gate)
    g = -jnp.exp(a_log_a) * jax.nn.softplus(a_gate + dt_bias_a)
    o_a, s_new = _gated_delta(qa, ka, va, g, beta, rec_state)
    o_a = _rms_norm(o_a, norm_g_a) * jax.nn.silu(z.reshape(bsz, t, H_A, DV_A))
    q_b = q_b.reshape(bsz, t, H_B, HD_B)
    k_b = k_b.reshape(bsz, t, KVH_B, HD_B)
    v_b = v_b.reshape(bsz, t, KVH_B, HD_B)
    qi_b = qi_b.reshape(bsz, t, HI_B, DI_B)
    o_b = sparse_attn(q_b, k_b, v_b, qi_b, ki_b, wi_b * IDX_SCALE)
    o = jnp.concatenate([o_a.reshape(bsz, t, -1), o_b.reshape(bsz, t, -1)], -1)
    return o, conv_new, s_new, k_b, v_b, ki_b


def _rglru_branch(xb, yb, conv_buf, h0, conv_w_c, conv_b_c, w_rg_a, b_rg_a, w_rg_x, b_rg_x, lambda_c):
    bsz, t, _ = xb.shape
    xc, conv_new = _causal_conv(xb, conv_buf, conv_w_c)
    xc = xc + conv_b_c
    xr = xc.reshape(bsz, t, NB_C, BW_C)
    r = jax.nn.sigmoid(jnp.einsum('btnd,nde->btne', xr, w_rg_a).reshape(bsz, t, D_C) + b_rg_a)
    i = jax.nn.sigmoid(jnp.einsum('btnd,nde->btne', xr, w_rg_x).reshape(bsz, t, D_C) + b_rg_x)
    log_a = -RG_C * r * jax.nn.softplus(-lambda_c)
    a = jnp.exp(log_a)
    bterm = jnp.sqrt(-jnp.expm1(2.0 * log_a)) * (i * xc)
    bterm = bterm.at[:, 0].add(a[:, 0] * h0)

    if t % SCAN_TT == 0:
        h = linear_scan(a, bterm)
    else:
        def comb(lhs, rhs):
            return lhs[0] * rhs[0], rhs[0] * lhs[1] + rhs[1]

        _, h = lax.associative_scan(comb, (a, bterm), axis=1)
    return h * jax.nn.gelu(yb), conv_new, h[:, -1]


def _mla_prompt(qn, qr, ckv, kr, w_uk, w_uv):
    bsz, s_len = qn.shape[:2]
    m = bsz * s_len
    ckv2 = ckv.reshape(m, R_KV)
    k_nope = matmul(ckv2, w_uk.reshape(R_KV, H_D * DN_D).astype(BF16), H_D * DN_D, BF16)
    v = matmul(ckv2, w_uv.reshape(R_KV, H_D * DV_D).astype(BF16), H_D * DV_D, BF16)
    qr_hm = jnp.moveaxis(qr.reshape(m, H_D, DR_D), 1, 0)
    out = mla_prompt(qn.reshape(m, H_D * DN_D), qr_hm, k_nope, kr.reshape(m, DR_D).astype(BF16), v, bsz, s_len)
    return out.reshape(bsz, s_len, H_D, DV_D)


def _mla_sample(qn, qr, ckv, kr, w_uk, w_uv, cache_ckv, cache_krope, page_table, g_count=SMP_G):
    bsz, t = qn.shape[:2]
    q_lat = jnp.einsum('bthd,rhd->bthr', qn, w_uk)
    rows = lambda a: jnp.swapaxes(a, 1, 2).reshape(bsz, H_D * t, a.shape[-1])
    lat = mla_sample_attn(page_table, rows(q_lat), rows(qr), _pad_page(ckv), jnp.swapaxes(_pad_page(kr), 1, 2),
                          cache_ckv, jnp.swapaxes(cache_krope, 1, 2), g_count)
    lat = jnp.swapaxes(lat.reshape(bsz, H_D, t, R_KV), 1, 2)
    return jnp.einsum('bthr,rhd->bthd', lat, w_uv)


def _mixer_cd_core(hm, hs, conv_buf, h0, pos0, latent_attn, conv_w_c, conv_b_c, w_rg_a, b_rg_a,
                   w_rg_x, b_rg_x, lambda_c, kv_norm_g_d):
    bsz, t, _ = hm.shape
    xb = hm[..., :D_C]
    yb = hm[..., D_C:2 * D_C]
    qn = hm[..., 2048:3072].reshape(bsz, t, H_D, DN_D)
    qr = hm[..., 3072:3584].reshape(bsz, t, H_D, DR_D)
    ckv = hm[..., 3584:3840]
    kr = hs[..., :DR_D]
    o_c, conv_new, h_new = _rglru_branch(xb, yb, conv_buf, h0, conv_w_c, conv_b_c, w_rg_a, b_rg_a,
                                         w_rg_x, b_rg_x, lambda_c)
    pos = pos0 + jnp.arange(t)
    qr = _rope(qr, pos)
    kr = _rope(kr, pos)
    ckv = _rms_norm(ckv, kv_norm_g_d)
    o_d = latent_attn(qn, qr, ckv, kr)
    o = jnp.concatenate([o_c, o_d.reshape(bsz, t, -1)], -1)
    return o, conv_new, h_new, ckv, kr


def _pad_cols(w, n):
    return jnp.pad(w, ((0, 0), (0, n - w.shape[1])))


def kernel(x_prompt, x_sample, state_a_conv, state_a_rec, cache_b_k, cache_b_v, cache_b_kidx, state_c_conv, state_c_h, cache_d_ckv, cache_d_krope, page_table, ln_g, ln_b, ffn_w_gate, ffn_w_up, ffn_w_down, w_in_ab, conv_w_a, a_log_a, dt_bias_a, norm_g_a, w_out_ab, w_in_cd, conv_w_c, conv_b_c, w_rg_a, b_rg_a, w_rg_x, b_rg_x, lambda_c, kv_norm_g_d, w_uk_d, w_uv_d, w_out_cd):
    bp, tp = x_prompt.shape[:2]
    bs, ts = x_sample.shape[:2]
    mp = bp * tp
    past = page_table.shape[1] * PAGE_SIZE
    n_pg = tp // PAGE_SIZE

    x = jnp.concatenate([x_prompt.reshape(mp, D_MODEL), x_sample.reshape(bs * ts, D_MODEL)], 0)

    wg = ffn_w_gate.astype(BF16)
    wu = ffn_w_up.astype(BF16)
    wd = ffn_w_down.astype(BF16)

    ab = np.cumsum((0, A_QKV, H_A * DV_A, H_A, H_A, H_B * HD_B, KVH_B * HD_B, KVH_B * HD_B, HI_B * DI_B, DI_B, HI_B))
    col = lambda w, i: w[:, int(ab[i]):int(ab[i + 1])]
    w_ab_main = jnp.concatenate([col(w_in_ab, 0), col(w_in_ab, 1), col(w_in_ab, 4), col(w_in_ab, 7),
                                 col(w_in_ab, 5), col(w_in_ab, 6)], 1).astype(BF16)
    w_ab_small = _pad_cols(jnp.concatenate([col(w_in_ab, 8), col(w_in_ab, 2), col(w_in_ab, 3), col(w_in_ab, 9)], 1),
                           LANE).astype(BF16)
    w_q_d = w_in_cd[:, 2 * D_C:2 * D_C + H_D * (DN_D + DR_D)].reshape(D_MODEL, H_D, DN_D + DR_D)
    w_cd_main = jnp.concatenate([w_in_cd[:, :2 * D_C],
                                 w_q_d[..., :DN_D].reshape(D_MODEL, H_D * DN_D),
                                 w_q_d[..., DN_D:].reshape(D_MODEL, H_D * DR_D),
                                 w_in_cd[:, 3584:3840]], 1).astype(BF16)
    w_cd_small = _pad_cols(w_in_cd[:, 3840:], LANE).astype(BF16)
    w_out_ab_b = w_out_ab.astype(BF16)
    w_out_cd_b = w_out_cd.astype(BF16)

    def ffn(x, layer, half, ln_idx):
        return ffn_ln(x, wg[layer, half], wu[layer, half], wd[layer, half],
                      ln_g[layer, ln_idx][None], ln_b[layer, ln_idx][None])

    x = ffn(x, 0, 0, 0)
    hm = matmul(x, w_ab_main, 1664)
    hs = matmul(x, w_ab_small, LANE)
    o_p, a_conv_p, a_rec_p, kb_p, vb_p, kib_p = _mixer_ab_core(
        hm[:mp].reshape(bp, tp, -1), hs[:mp].reshape(bp, tp, -1),
        jnp.zeros((bp, CONV_W - 1, A_QKV), F32), jnp.zeros((bp, H_A, DK_A, DV_A), F32),
        lambda *unused: dsa_prompt(hm, hs, bp, tp), conv_w_a, a_log_a, dt_bias_a, norm_g_a)
    attn_s = functools.partial(_dsa_sample, cache_k=cache_b_k, cache_v=cache_b_v,
                               cache_kidx=cache_b_kidx, page_table=page_table)
    o_s, a_conv_s, a_rec_s, b_k_s, b_v_s, b_kidx_s = _mixer_ab_core(
        hm[mp:].reshape(bs, ts, -1), hs[mp:].reshape(bs, ts, -1), state_a_conv, state_a_rec,
        attn_s, conv_w_a, a_log_a, dt_bias_a, norm_g_a)
    b_k_p = kb_p.reshape(bp, n_pg, PAGE_SIZE, KVH_B, HD_B)
    b_v_p = vb_p.reshape(bp, n_pg, PAGE_SIZE, KVH_B, HD_B)
    b_kidx_p = kib_p.reshape(bp, n_pg, PAGE_SIZE, DI_B)
    o = jnp.concatenate([o_p.reshape(mp, -1), o_s.reshape(bs * ts, -1)], 0)
    x = proj_ln(x, o, w_out_ab_b, ln_g[0, 1][None], ln_b[0, 1][None])
    x = ffn(x, 0, 1, 2)

    x = ffn(x, 1, 0, 0)
    hm = matmul(x, w_cd_main, 1920)
    hs = matmul(x, w_cd_small, LANE)
    attn_p = functools.partial(_mla_prompt, w_uk=w_uk_d, w_uv=w_uv_d)
    o_p, c_conv_p, c_h_p, ckv_p, kr_p = _mixer_cd_core(
        hm[:mp].reshape(bp, tp, -1), hs[:mp].reshape(bp, tp, -1),
        jnp.zeros((bp, CONV_W - 1, D_C), F32), jnp.zeros((bp, D_C), F32), 0, attn_p,
        conv_w_c, conv_b_c, w_rg_a, b_rg_a, w_rg_x, b_rg_x, lambda_c, kv_norm_g_d)
    attn_s = functools.partial(_mla_sample, w_uk=w_uk_d, w_uv=w_uv_d, cache_ckv=cache_d_ckv,
                               cache_krope=cache_d_krope, page_table=page_table)
    o_s, c_conv_s, c_h_s, d_ckv_s, d_krope_s = _mixer_cd_core(
        hm[mp:].reshape(bs, ts, -1), hs[mp:].reshape(bs, ts, -1), state_c_conv, state_c_h, past, attn_s,
        conv_w_c, conv_b_c, w_rg_a, b_rg_a, w_rg_x, b_rg_x, lambda_c, kv_norm_g_d)
    d_ckv_p = ckv_p.reshape(bp, n_pg, PAGE_SIZE, R_KV)
    d_krope_p = kr_p.reshape(bp, n_pg, PAGE_SIZE, DR_D)
    o = jnp.concatenate([o_p.reshape(mp, -1), o_s.reshape(bs * ts, -1)], 0)
    x = proj_ln(x, o, w_out_cd_b, ln_g[1, 1][None], ln_b[1, 1][None])
    x = ffn(x, 1, 1, 2)

    yp = x[:mp].reshape(bp, tp, D_MODEL)
    ys = x[mp:].reshape(bs, ts, D_MODEL)
    return (yp, ys,
            a_conv_p, a_conv_s, a_rec_p, a_rec_s,
            b_k_p, b_k_s, b_v_p, b_v_s, b_kidx_p, b_kidx_s,
            c_conv_p, c_conv_s, c_h_p, c_h_s,
            d_ckv_p, d_ckv_s, d_krope_p, d_krope_s)
```
